```python
import math
import jax, jax.numpy as jnp
from jax import lax
import numpy as np

D_MODEL = 1024
BATCH = 8
SEQ = 2048
DEPTH = 2

CHUNK = 64
N_MEM = 256
D_MIX = D_MODEL
RET_HEADS = 4
RET_HD = 128
RET_WIDTH = RET_HEADS * RET_HD
SB_HEADS = 8
SB_HD = 64
SB_WIDTH = SB_HEADS * SB_HD
IN_COLS = 4 * RET_WIDTH + 3 * SB_WIDTH
X_HEADS = 4
X_HD = D_MODEL // X_HEADS
D_FF = 4 * D_MODEL
SB_BLOCK = 128
EPS = 1e-6
ROPE_BASE = 10000.0

kernel_name = "hymba_retention_stickbreaking_block"


def _rmsnorm(x, g):
    xf = x.astype(jnp.float32)
    y = xf * lax.rsqrt(jnp.mean(xf * xf, axis=-1, keepdims=True) + EPS)
    return (y * g.astype(jnp.float32)).astype(x.dtype)


def _rotary(x):
    s, hd = x.shape[1], x.shape[-1]
    half = hd // 2
    inv = 1.0 / (ROPE_BASE ** jnp.linspace(0.0, 1.0, half, dtype=jnp.float32))
    ang = jnp.arange(s, dtype=jnp.float32)[:, None] * inv[None, :]
    cos = jnp.cos(ang)[None, :, None, :]
    sin = jnp.sin(ang)[None, :, None, :]
    xf = x.astype(jnp.float32)
    x1, x2 = xf[..., :half], xf[..., half:]
    return jnp.concatenate([x1 * cos - x2 * sin, x1 * sin + x2 * cos], axis=-1).astype(x.dtype)


def _retention(q, k, v):
    b, s, h, d = q.shape
    nc = s // CHUNK
    f32 = jnp.float32
    log_gamma = jnp.log1p(-jnp.exp2(-5.0 - jnp.arange(h, dtype=f32)))
    qf = q.astype(f32).reshape(b, nc, CHUNK, h, d)
    kf = k.astype(f32).reshape(b, nc, CHUNK, h, d) * (d ** -0.5)
    vf = v.astype(f32).reshape(b, nc, CHUNK, h, d)
    idx = jnp.arange(CHUNK, dtype=f32)
    dist = jnp.abs(idx[:, None] - idx[None, :])
    dmask = jnp.exp(log_gamma[:, None, None] * dist)
    scores = jnp.einsum('bnihd,bnjhd->bnhij', qf, kf) * dmask
    o_intra = jnp.einsum('bnhij,bnjhe->bnihe', scores, vf)
    k_dec = jnp.exp(log_gamma[None, :] * (CHUNK - 1.0 - idx)[:, None])
    u = jnp.einsum('bnjhd,bnjhe->nbhde', kf * k_dec[:, :, None], vf)
    chunk_decay = jnp.exp(log_gamma * CHUNK)[None, :, None, None]

    def step(state, u_c):
        return state * chunk_decay + u_c, state

    _, s_before = lax.scan(step, jnp.zeros_like(u[0]), u)
    q_dec = jnp.exp(log_gamma[None, :] * (idx + 1.0)[:, None])
    o_inter = jnp.einsum('bnihd,nbhde->bnihe', qf * q_dec[:, :, None], s_before)
    return (o_intra + o_inter).reshape(b, s, h, d)


def _stick_breaking(q, k, v):
    b, s, h, d = q.shape
    scale = d ** -0.5
    outs = []
    for blk in range(s // SB_BLOCK):
        q0 = blk * SB_BLOCK
        end = q0 + SB_BLOCK
        qb = q[:, q0:end].astype(jnp.float32)
        kb = k[:, :end].astype(jnp.float32)
        vb = v[:, :end].astype(jnp.float32)
        z = jnp.einsum('bthd,bshd->bhts', qb, kb) * scale
        t_idx = q0 + jnp.arange(SB_BLOCK)[:, None]
        s_idx = jnp.arange(end)[None, :]
        strict = s_idx < t_idx
        log_beta = jax.nn.log_sigmoid(z)
        log_1m = jnp.where(strict, jax.nn.log_sigmoid(-z), 0.0)
        after = lax.cumsum(log_1m, axis=3, reverse=True) - log_1m
        a = jnp.where(strict, jnp.exp(log_beta + after), 0.0)
        outs.append(jnp.einsum('bhts,bshd->bthd', a, vb))
    return jnp.concatenate(outs, axis=1)


def setup_inputs(seed: int = 0) -> dict:
    key = jax.random.key(seed)
    ks = jax.random.split(key, 20)
    f32 = jnp.float32

    def w(k, shape, fan_in, gain=1.0):
        return jax.random.normal(k, shape, f32) * (gain * fan_in ** -0.5)

    def g(k, shape):
        return 1.0 + 0.02 * jax.random.normal(k, shape, f32)

    return {
        "x": jax.random.normal(ks[0], (BATCH, SEQ, D_MODEL), f32),
        "mem": jax.random.normal(ks[1], (BATCH, N_MEM, D_MODEL), f32),
        "g_mix": g(ks[2], (DEPTH, D_MODEL)),
        "w_in": w(ks[3], (DEPTH, D_MODEL, IN_COLS), D_MODEL),
        "g_ret_out": g(ks[4], (DEPTH, RET_HEADS, RET_HD)),
        "g_sb_out": g(ks[5], (DEPTH, SB_WIDTH)),
        "w_mix_out": w(ks[6], (DEPTH, D_MIX, D_MODEL), D_MIX, 0.5),
        "g_cross": g(ks[7], (DEPTH, D_MODEL)),
        "g_mem": g(ks[8], (DEPTH, D_MODEL)),
        "w_xq": w(ks[9], (DEPTH, D_MODEL, D_MODEL), D_MODEL),
        "w_xkv": w(ks[10], (DEPTH, D_MODEL, 2 * D_MODEL), D_MODEL),
        "g_qn": g(ks[11], (DEPTH, X_HD)),
        "g_kn": g(ks[12], (DEPTH, X_HD)),
        "w_xo": w(ks[13], (DEPTH, D_MODEL, D_MODEL), D_MODEL, 0.5),
        "g_mlp": g(ks[14], (DEPTH, D_MODEL)),
        "w_up": w(ks[15], (DEPTH, D_MODEL, D_FF), D_MODEL),
        "w_down": w(ks[16], (DEPTH, D_FF, D_MODEL), D_FF, 0.5),
    }


def reference(x, mem, g_mix, w_in, g_ret_out, g_sb_out, w_mix_out, g_cross, g_mem,
              w_xq, w_xkv, g_qn, g_kn, w_xo, g_mlp, w_up, w_down):
    b, s, _ = x.shape
    m = mem.shape[1]
    splits = [RET_WIDTH, 2 * RET_WIDTH, 3 * RET_WIDTH, 4 * RET_WIDTH,
              4 * RET_WIDTH + SB_WIDTH, 4 * RET_WIDTH + 2 * SB_WIDTH]
    for l in range(DEPTH):
        h = _rmsnorm(x, g_mix[l])
        proj = h @ w_in[l]
        rq, rk, rv, rg, sq, sk, sv = jnp.split(proj, splits, axis=-1)
        rq = _rotary(rq.reshape(b, s, RET_HEADS, RET_HD))
        rk = _rotary(rk.reshape(b, s, RET_HEADS, RET_HD))
        rv = rv.reshape(b, s, RET_HEADS, RET_HD)
        o_ret = _rmsnorm(_retention(rq, rk, rv), g_ret_out[l])
        o_ret = o_ret.reshape(b, s, RET_WIDTH).astype(x.dtype) * jax.nn.silu(rg)
        o_sb = _stick_breaking(sq.reshape(b, s, SB_HEADS, SB_HD),
                               sk.reshape(b, s, SB_HEADS, SB_HD),
                               sv.reshape(b, s, SB_HEADS, SB_HD))
        o_sb = _rmsnorm(o_sb.reshape(b, s, SB_WIDTH), g_sb_out[l]).astype(x.dtype)
        x = x + jnp.concatenate([o_ret, o_sb], axis=-1) @ w_mix_out[l]

        hq = _rmsnorm(x, g_cross[l])
        mn = _rmsnorm(mem, g_mem[l])
        q = _rmsnorm((hq @ w_xq[l]).reshape(b, s, X_HEADS, X_HD), g_qn[l])
        kv = mn @ w_xkv[l]
        k, v = jnp.split(kv, 2, axis=-1)
        k = _rmsnorm(k.reshape(b, m, X_HEADS, X_HD), g_kn[l])
        v = v.reshape(b, m, X_HEADS, X_HD)
        sc = jnp.einsum('bthd,bmhd->bhtm', q, k).astype(jnp.float32) * (X_HD ** -0.5)
        p = jax.nn.softmax(sc, axis=-1).astype(v.dtype)
        o = jnp.einsum('bhtm,bmhd->bthd', p, v).reshape(b, s, D_MODEL)
        x = x + o @ w_xo[l]

        hm = _rmsnorm(x, g_mlp[l])
        x = x + jnp.square(jax.nn.relu(hm @ w_up[l])) @ w_down[l]
    return x
```

```python
import functools
import math

import jax
import jax.numpy as jnp
from jax import lax
from jax.experimental import pallas as pl
from jax.experimental.pallas import tpu as pltpu

F32 = jnp.float32
BF16 = jnp.bfloat16

EPS = 1e-6
ROPE_BASE = 10000.0
CHUNK = 64
RET_HEADS = 4
RET_HD = 128
RET_WIDTH = RET_HEADS * RET_HD
SB_HEADS = 8
SB_HD = 64
SB_WIDTH = SB_HEADS * SB_HD
X_HEADS = 4
LANES = 128
ROW_TILE = 512
RET_BLOCK = 256
SB_BLOCK = 128
FF_CHUNK = 512
VMEM_LIMIT = 56 * 1024 * 1024


def _nt_dot(a, b):
    return lax.dot_general(a, b, (((1,), (1,)), ((), ())), preferred_element_type=F32)


def _dot(a, b):
    return jnp.dot(a, b, preferred_element_type=F32)


def _rms(x, g):
    return x * lax.rsqrt(jnp.mean(x * x, axis=-1, keepdims=True) + EPS) * g


def _params(*sem):
    return pltpu.CompilerParams(dimension_semantics=sem, vmem_limit_bytes=VMEM_LIMIT)


def _in_proj_kernel(x_ref, g_ref, w_ref, cos_ref, sin_ref,
                    rq_ref, rk_ref, rv_ref, rg_ref, sq_ref, sk_ref, sv_ref):
    h = _rms(x_ref[...], g_ref[...]).astype(BF16)
    cos = cos_ref[...]
    sin = sin_ref[...]

    def proj(i):
        return _dot(h, w_ref[:, i * RET_WIDTH:(i + 1) * RET_WIDTH])

    def rotary(p, scale):
        for hd in range(RET_HEADS):
            c = p[:, hd * RET_HD:(hd + 1) * RET_HD]
            r = c * cos + pltpu.roll(c, RET_HD // 2, 1) * sin
            yield hd, r * scale if scale != 1.0 else r

    for hd, r in rotary(proj(0), 1.0):
        rq_ref[:, hd * RET_HD:(hd + 1) * RET_HD] = r.astype(BF16)
    for hd, r in rotary(proj(1), RET_HD ** -0.5):
        rk_ref[:, hd * RET_HD:(hd + 1) * RET_HD] = r.astype(BF16)
    rv_ref[...] = proj(2).astype(BF16)
    rg_ref[...] = proj(3)
    sq_ref[...] = (proj(4) * (SB_HD ** -0.5)).astype(BF16)
    sk_ref[...] = proj(5).astype(BF16)
    sv_ref[...] = proj(6).astype(BF16)


def _in_proj(x, g, w, cos, sin, seq):
    t, d = x.shape
    n_pos = seq // ROW_TILE
    row = lambda i: (i, 0)
    const = lambda i: (0, 0)
    half = pl.BlockSpec((ROW_TILE, RET_WIDTH), row)
    pos = pl.BlockSpec((ROW_TILE, RET_HD), lambda i: (i % n_pos, 0))
    outs = [jax.ShapeDtypeStruct((t, RET_WIDTH), dt)
            for dt in (BF16, BF16, BF16, F32, BF16, BF16, BF16)]
    return pl.pallas_call(
        _in_proj_kernel,
        grid=(t // ROW_TILE,),
        in_specs=[pl.BlockSpec((ROW_TILE, d), row), pl.BlockSpec((1, d), const),
                  pl.BlockSpec(w.shape, const), pos, pos],
        out_specs=[half] * 7,
        out_shape=outs,
        compiler_params=_params("parallel"),
        name="in_proj",
    )(x, g, w, cos, sin)


def _retention_kernel(q_ref, k_ref, v_ref, gate_ref, w_ref, qd_ref, kd_ref, cd_ref, gn_ref,
                      o_ref, state_ref):
    @pl.when(pl.program_id(2) == 0)
    def _():
        state_ref[...] = jnp.zeros_like(state_ref)

    q = q_ref[...]
    k = k_ref[...]
    v = v_ref[...]
    state = state_ref[...]
    p = (_nt_dot(q, k) * w_ref[0]).astype(BF16)
    o = _dot(p, v) + _dot(q, state.astype(BF16)) * qd_ref[0]
    kd = (k.astype(F32) * kd_ref[0]).T.astype(BF16)
    state_ref[...] = state * cd_ref[0] + _dot(kd, v)
    gate = gate_ref[...]
    o_ref[...] = (_rms(o, gn_ref[0]) * (gate * jax.nn.sigmoid(gate))).astype(BF16)


def _retention(rq, rk, rv, rg, tabs, gn, batch, seq):
    t = rq.shape[0]
    nb = seq // RET_BLOCK
    w, qd, kd, cd = tabs
    tok = pl.BlockSpec((RET_BLOCK, RET_HD), lambda b, h, i: (b * nb + i, h))
    head = lambda shape: pl.BlockSpec((1,) + shape, lambda b, h, i: (h, 0, 0))
    return pl.pallas_call(
        _retention_kernel,
        grid=(batch, RET_HEADS, nb),
        in_specs=[tok, tok, tok, tok, head((RET_BLOCK, RET_BLOCK)), head((RET_BLOCK, RET_HD)),
                  head((RET_BLOCK, RET_HD)), head((1, RET_HD)), head((1, RET_HD))],
        out_specs=tok,
        out_shape=jax.ShapeDtypeStruct((t, RET_WIDTH), BF16),
        scratch_shapes=[pltpu.VMEM((RET_HD, RET_HD), F32)],
        compiler_params=_params("parallel", "parallel", "arbitrary"),
        name="retention",
    )(rq, rk, rv, rg, w, qd, kd, cd, gn)


def _retention_tables():
    lg = jnp.log1p(-jnp.exp2(-5.0 - jnp.arange(RET_HEADS, dtype=F32)))[:, None, None]
    idx = jnp.arange(RET_BLOCK, dtype=F32)
    dist = jnp.abs(idx[:, None] - idx[None, :])
    chunk = jnp.arange(RET_BLOCK) // CHUNK
    visible = chunk[None, :] <= chunk[:, None]
    w = jnp.where(visible[None], jnp.exp(lg * dist[None]), 0.0)
    ones = jnp.ones((1, 1, RET_HD), F32)
    qd = jnp.exp(lg * (idx + 1.0)[None, :, None]) * ones
    kd = jnp.exp(lg * (RET_BLOCK - 1.0 - idx)[None, :, None]) * ones
    cd = jnp.exp(lg * RET_BLOCK) * ones
    return w, qd, kd, cd


def _stick_breaking_kernel(q_ref, k_ref, v_ref, mo_ref, o_ref, acc_ref, carry_ref):
    qi = pl.program_id(2)
    lane = lax.broadcasted_iota(jnp.int32, (SB_BLOCK, LANES), 1)
    first = lane < SB_HD
    q = q_ref[...].astype(F32)
    q2 = jnp.concatenate([jnp.where(first, q, 0.0), jnp.where(first, 0.0, q)], axis=0).astype(BF16)
    mo = mo_ref[...]
    row = lax.broadcasted_iota(jnp.int32, (2 * SB_BLOCK, SB_BLOCK), 0) % SB_BLOCK
    col = lax.broadcasted_iota(jnp.int32, (2 * SB_BLOCK, SB_BLOCK), 1)
    strict = col < row

    def tile(kb, diag):
        start = pl.multiple_of(kb * SB_BLOCK, SB_BLOCK)
        z = _nt_dot(q2, k_ref[pl.ds(start, SB_BLOCK), :])
        soft = jnp.log1p(jnp.exp(-jnp.abs(z)))
        log_beta = jnp.minimum(z, 0.0) - soft
        log_1m = jnp.minimum(-z, 0.0) - soft
        if diag:
            log_1m = jnp.where(strict, log_1m, 0.0)
        hi = log_1m.astype(BF16)
        lo = (log_1m - hi.astype(F32)).astype(BF16)
        cs = _dot(jnp.concatenate([hi, lo], axis=1), mo)
        after = cs[:, :SB_BLOCK]
        if not diag:
            after = after + carry_ref[...]
        a = jnp.exp(log_beta + after)
        if diag:
            a = jnp.where(strict, a, 0.0)
        pv = _dot(a.astype(BF16), v_ref[pl.ds(start, SB_BLOCK), :])
        if diag:
            acc_ref[...] = pv
            carry_ref[...] = cs[:, SB_BLOCK:]
        else:
            acc_ref[...] += pv
            carry_ref[...] += cs[:, SB_BLOCK:]

    tile(qi, True)

    def body(j, c):
        tile(qi - 1 - j, False)
        return c

    lax.fori_loop(0, qi, body, 0)
    o_ref[...] = jnp.where(first, acc_ref[:SB_BLOCK, :], acc_ref[SB_BLOCK:, :])


def _stick_breaking(sq, sk, sv, mo, batch, seq):
    t = sq.shape[0]
    nq = seq // SB_BLOCK
    pairs = SB_WIDTH // LANES
    qspec = pl.BlockSpec((SB_BLOCK, LANES), lambda b, p, i: (b * nq + i, p))
    kspec = pl.BlockSpec((seq, LANES), lambda b, p, i: (b, p))
    return pl.pallas_call(
        _stick_breaking_kernel,
        grid=(batch, pairs, nq),
        in_specs=[qspec, kspec, kspec, pl.BlockSpec(mo.shape, lambda b, p, i: (0, 0))],
        out_specs=qspec,
        out_shape=jax.ShapeDtypeStruct((t, SB_WIDTH), F32),
        scratch_shapes=[pltpu.VMEM((2 * SB_BLOCK, LANES), F32)] * 2,
        compiler_params=_params("parallel", "parallel", "arbitrary"),
        name="stick_breaking",
    )(sq, sk, sv, mo)


def _cumsum_matrix():
    j = jnp.arange(SB_BLOCK)[:, None]
    s = jnp.arange(SB_BLOCK)[None, :]
    tri = (j > s).astype(BF16)
    half = jnp.concatenate([tri, jnp.ones((SB_BLOCK, SB_BLOCK), BF16)], axis=1)
    return jnp.concatenate([half, half], axis=0)


def _mem_kv_kernel(m_ref, g_ref, w_ref, gk_ref, k_ref, v_ref):
    d = m_ref.shape[1]
    hd = d // X_HEADS
    mn = _rms(m_ref[...], g_ref[...]).astype(BF16)
    kv = _dot(mn, w_ref[...])
    for h in range(X_HEADS):
        kh = kv[:, h * hd:(h + 1) * hd]
        k_ref[:, h * hd:(h + 1) * hd] = (_rms(kh, gk_ref[...]) * (hd ** -0.5)).astype(BF16)
    v_ref[...] = kv[:, d:].astype(BF16)


def _mem_kv(mem, g, w, gk):
    t, d = mem.shape
    row = lambda i: (i, 0)
    const = lambda i: (0, 0)
    out = pl.BlockSpec((ROW_TILE, d), row)
    return pl.pallas_call(
        _mem_kv_kernel,
        grid=(t // ROW_TILE,),
        in_specs=[pl.BlockSpec((ROW_TILE, d), row), pl.BlockSpec((1, d), const),
                  pl.BlockSpec(w.shape, const), pl.BlockSpec(gk.shape, const)],
        out_specs=[out, out],
        out_shape=[jax.ShapeDtypeStruct((t, d), BF16)] * 2,
        compiler_params=_params("parallel"),
        name="mem_kv",
    )(mem, g, w, gk)


def _mix_cross_kernel(x_ref, oret_ref, osb_ref, gsb_ref, wmix_ref, gc_ref, wxq_ref, gq_ref,
                      k_ref, v_ref, wxo_ref, o_ref):
    d = x_ref.shape[1]
    hd = d // X_HEADS
    nsb = _rms(osb_ref[...], gsb_ref[...]).astype(BF16)
    x = (x_ref[...] + _dot(oret_ref[...], wmix_ref[:RET_WIDTH, :])
         + _dot(nsb, wmix_ref[RET_WIDTH:, :]))
    q = _dot(_rms(x, gc_ref[...]).astype(BF16), wxq_ref[...])
    heads = []
    for h in range(X_HEADS):
        cols = slice(h * hd, (h + 1) * hd)
        qn = _rms(q[:, cols], gq_ref[...]).astype(BF16)
        s = _nt_dot(qn, k_ref[0, :, cols])
        e = jnp.exp(s - jnp.max(s, axis=-1, keepdims=True))
        p = (e / jnp.sum(e, axis=-1, keepdims=True)).astype(BF16)
        heads.append(_dot(p, v_ref[0, :, cols]).astype(BF16))
    o_ref[...] = x + _dot(jnp.concatenate(heads, axis=1), wxo_ref[...])


def _mix_cross(x, oret, osb, gsb, wmix, gc, wxq, gq, k, v, wxo, seq):
    t, d = x.shape
    per_batch = seq // ROW_TILE
    row = lambda i: (i, 0)
    const = lambda i: (0, 0)
    full = lambda a: pl.BlockSpec(a.shape, const)
    mem = pl.BlockSpec((1,) + k.shape[1:], lambda i: (i // per_batch, 0, 0))
    return pl.pallas_call(
        _mix_cross_kernel,
        grid=(t // ROW_TILE,),
        in_specs=[pl.BlockSpec((ROW_TILE, d), row), pl.BlockSpec((ROW_TILE, RET_WIDTH), row),
                  pl.BlockSpec((ROW_TILE, SB_WIDTH), row), full(gsb), full(wmix), full(gc),
                  full(wxq), full(gq), mem, mem, full(wxo)],
        out_specs=pl.BlockSpec((ROW_TILE, d), row),
        out_shape=jax.ShapeDtypeStruct((t, d), F32),
        compiler_params=_params("parallel"),
        name="mix_cross",
    )(x, oret, osb, gsb, wmix, gc, wxq, gq, k, v, wxo)


def _mlp_kernel(x_ref, g_ref, wup_ref, wdn_ref, o_ref, h_ref):
    x = x_ref[...]
    hm = _rms(x, g_ref[...]).astype(BF16)
    for f in range(h_ref.shape[1] // FF_CHUNK):
        cols = slice(f * FF_CHUNK, (f + 1) * FF_CHUNK)
        up = _dot(hm, wup_ref[:, cols])
        h_ref[:, cols] = jnp.square(jnp.maximum(up, 0.0)).astype(BF16)
    o_ref[...] = x + _dot(h_ref[...], wdn_ref[...])


def _mlp(x, g, wup, wdn):
    t, d = x.shape
    row = lambda i: (i, 0)
    const = lambda i: (0, 0)
    return pl.pallas_call(
        _mlp_kernel,
        grid=(t // ROW_TILE,),
        in_specs=[pl.BlockSpec((ROW_TILE, d), row), pl.BlockSpec((1, d), const),
                  pl.BlockSpec(wup.shape, const), pl.BlockSpec(wdn.shape, const)],
        out_specs=pl.BlockSpec((ROW_TILE, d), row),
        out_shape=jax.ShapeDtypeStruct((t, d), F32),
        scratch_shapes=[pltpu.VMEM((ROW_TILE, wup.shape[1]), BF16)],
        compiler_params=_params("parallel"),
        name="mlp",
    )(x, g, wup, wdn)


def _rotary_tables(seq):
    half = RET_HD // 2
    inv = 1.0 / (ROPE_BASE ** jnp.linspace(0.0, 1.0, half, dtype=F32))
    ang = jnp.arange(seq, dtype=F32)[:, None] * inv[None, :]
    cos, sin = jnp.cos(ang), jnp.sin(ang)
    return jnp.concatenate([cos, cos], axis=1), jnp.concatenate([-sin, sin], axis=1)


def kernel(x, mem, g_mix, w_in, g_ret_out, g_sb_out, w_mix_out, g_cross, g_mem,
           w_xq, w_xkv, g_qn, g_kn, w_xo, g_mlp, w_up, w_down):
    batch, seq, d = x.shape
    n_mem = mem.shape[1]
    depth = w_in.shape[0]
    cos, sin = _rotary_tables(seq)
    ret_tabs = _retention_tables()
    mo = _cumsum_matrix()
    xt = x.reshape(batch * seq, d)
    memt = mem.reshape(batch * n_mem, d)
    vec = lambda a: a.reshape(1, -1)
    for l in range(depth):
        rq, rk, rv, rg, sq, sk, sv = _in_proj(xt, vec(g_mix[l]), w_in[l].astype(BF16), cos, sin, seq)
        o_ret = _retention(rq, rk, rv, rg, ret_tabs, g_ret_out[l].reshape(RET_HEADS, 1, RET_HD),
                           batch, seq)
        o_sb = _stick_breaking(sq, sk, sv, mo, batch, seq)
        km, vm = _mem_kv(memt, vec(g_mem[l]), w_xkv[l].astype(BF16), vec(g_kn[l]))
        xt = _mix_cross(xt, o_ret, o_sb, vec(g_sb_out[l]), w_mix_out[l].astype(BF16),
                        vec(g_cross[l]), w_xq[l].astype(BF16), vec(g_qn[l]),
                        km.reshape(batch, n_mem, d), vm.reshape(batch, n_mem, d),
                        w_xo[l].astype(BF16), seq)
        xt = _mlp(xt, vec(g_mlp[l]), w_up[l].astype(BF16), w_down[l].astype(BF16))
    return xt.reshape(batch, seq, d)
```

```python
import functools
import math

import jax
import jax.numpy as jnp
from jax import lax
from jax.experimental import pallas as pl
from jax.experimental.pallas import tpu as pltpu

F32 = jnp.float32
BF16 = jnp.bfloat16

EPS = 1e-6
LOG2E = math.log2(math.e)
ROPE_BASE = 10000.0
CHUNK = 64
RET_HEADS = 4
RET_HD = 128
RET_WIDTH = RET_HEADS * RET_HD
SB_HEADS = 8
SB_HD = 64
SB_WIDTH = SB_HEADS * SB_HD
X_HEADS = 4
LANES = 128
ROW_TILE = 512
RET_BLOCK = 256
SB_BLOCK = 128
FF_CHUNK = 512
VMEM_LIMIT = 56 * 1024 * 1024


def _nt_dot(a, b):
    return lax.dot_general(a, b, (((1,), (1,)), ((), ())), preferred_element_type=F32)


def _dot(a, b):
    return jnp.dot(a, b, preferred_element_type=F32)


def _rms(x, g):
    return x * lax.rsqrt(jnp.mean(x * x, axis=-1, keepdims=True) + EPS) * g


def _params(*sem):
    return pltpu.CompilerParams(dimension_semantics=sem, vmem_limit_bytes=VMEM_LIMIT)


def _in_proj_kernel(x_ref, g_ref, w_ref, cos_ref, sin_ref,
                    rq_ref, rk_ref, rv_ref, rg_ref, sq_ref, sk_ref, sv_ref):
    h = _rms(x_ref[...], g_ref[...]).astype(BF16)
    cos = cos_ref[...]
    sin = sin_ref[...]

    def proj(i):
        return _dot(h, w_ref[:, i * RET_WIDTH:(i + 1) * RET_WIDTH])

    def rotary(p, scale):
        for hd in range(RET_HEADS):
            c = p[:, hd * RET_HD:(hd + 1) * RET_HD]
            r = c * cos + pltpu.roll(c, RET_HD // 2, 1) * sin
            yield hd, r * scale if scale != 1.0 else r

    for hd, r in rotary(proj(0), 1.0):
        rq_ref[:, hd * RET_HD:(hd + 1) * RET_HD] = r.astype(BF16)
    for hd, r in rotary(proj(1), RET_HD ** -0.5):
        rk_ref[:, hd * RET_HD:(hd + 1) * RET_HD] = r.astype(BF16)
    rv_ref[...] = proj(2).astype(BF16)
    rg_ref[...] = proj(3)
    sq_ref[...] = (proj(4) * (SB_HD ** -0.5)).astype(BF16)
    sk_ref[...] = proj(5).astype(BF16)
    sv_ref[...] = proj(6).astype(BF16)


def _in_proj(x, g, w, cos, sin, seq):
    t, d = x.shape
    n_pos = seq // ROW_TILE
    row = lambda i: (i, 0)
    const = lambda i: (0, 0)
    half = pl.BlockSpec((ROW_TILE, RET_WIDTH), row)
    pos = pl.BlockSpec((ROW_TILE, RET_HD), lambda i: (i % n_pos, 0))
    outs = [jax.ShapeDtypeStruct((t, RET_WIDTH), dt)
            for dt in (BF16, BF16, BF16, F32, BF16, BF16, BF16)]
    return pl.pallas_call(
        _in_proj_kernel,
        grid=(t // ROW_TILE,),
        in_specs=[pl.BlockSpec((ROW_TILE, d), row), pl.BlockSpec((1, d), const),
                  pl.BlockSpec(w.shape, const), pos, pos],
        out_specs=[half] * 7,
        out_shape=outs,
        compiler_params=_params("parallel"),
        name="in_proj",
    )(x, g, w, cos, sin)


def _retention_kernel(q_ref, k_ref, v_ref, gate_ref, w_ref, qd_ref, kd_ref, cd_ref, gn_ref,
                      o_ref, state_ref):
    @pl.when(pl.program_id(2) == 0)
    def _():
        state_ref[...] = jnp.zeros_like(state_ref)

    q = q_ref[...]
    k = k_ref[...]
    v = v_ref[...]
    state = state_ref[...]
    p = (_nt_dot(q, k) * w_ref[0]).astype(BF16)
    o = _dot(p, v) + _dot(q, state.astype(BF16)) * qd_ref[0]
    kd = (k.astype(F32) * kd_ref[0]).T.astype(BF16)
    state_ref[...] = state * cd_ref[0] + _dot(kd, v)
    gate = gate_ref[...]
    o_ref[...] = (_rms(o, gn_ref[0]) * (gate * jax.nn.sigmoid(gate))).astype(BF16)


def _retention(rq, rk, rv, rg, tabs, gn, batch, seq):
    t = rq.shape[0]
    nb = seq // RET_BLOCK
    w, qd, kd, cd = tabs
    tok = pl.BlockSpec((RET_BLOCK, RET_HD), lambda b, h, i: (b * nb + i, h))
    head = lambda shape: pl.BlockSpec((1,) + shape, lambda b, h, i: (h, 0, 0))
    return pl.pallas_call(
        _retention_kernel,
        grid=(batch, RET_HEADS, nb),
        in_specs=[tok, tok, tok, tok, head((RET_BLOCK, RET_BLOCK)), head((RET_BLOCK, RET_HD)),
                  head((RET_BLOCK, RET_HD)), head((1, RET_HD)), head((1, RET_HD))],
        out_specs=tok,
        out_shape=jax.ShapeDtypeStruct((t, RET_WIDTH), BF16),
        scratch_shapes=[pltpu.VMEM((RET_HD, RET_HD), F32)],
        compiler_params=_params("parallel", "parallel", "arbitrary"),
        name="retention",
    )(rq, rk, rv, rg, w, qd, kd, cd, gn)


def _retention_tables():
    lg = jnp.log1p(-jnp.exp2(-5.0 - jnp.arange(RET_HEADS, dtype=F32)))[:, None, None]
    idx = jnp.arange(RET_BLOCK, dtype=F32)
    dist = jnp.abs(idx[:, None] - idx[None, :])
    chunk = jnp.arange(RET_BLOCK) // CHUNK
    visible = chunk[None, :] <= chunk[:, None]
    w = jnp.where(visible[None], jnp.exp(lg * dist[None]), 0.0)
    ones = jnp.ones((1, 1, RET_HD), F32)
    qd = jnp.exp(lg * (idx + 1.0)[None, :, None]) * ones
    kd = jnp.exp(lg * (RET_BLOCK - 1.0 - idx)[None, :, None]) * ones
    cd = jnp.exp(lg * RET_BLOCK) * ones
    return w, qd, kd, cd


def _stick_breaking_kernel(q_ref, k_ref, v_ref, mo_ref, o_ref, q2_ref, acc_ref, carry_ref):
    nq = q_ref.shape[0] // SB_BLOCK
    p = pl.program_id(2)
    blocks = (p, nq - 1 - p)
    lane = lax.broadcasted_iota(jnp.int32, (SB_BLOCK, LANES), 1)
    first = lane < SB_HD
    for s, blk in enumerate(blocks):
        q = q_ref[pl.ds(pl.multiple_of(blk * SB_BLOCK, SB_BLOCK), SB_BLOCK), :].astype(F32)
        q2_ref[s] = jnp.concatenate([jnp.where(first, q, 0.0), jnp.where(first, 0.0, q)],
                                    axis=0).astype(BF16)
    row = lax.broadcasted_iota(jnp.int32, (2 * SB_BLOCK, SB_BLOCK), 0) % SB_BLOCK
    col = lax.broadcasted_iota(jnp.int32, (2 * SB_BLOCK, SB_BLOCK), 1)
    strict = col < row

    tiles = [(0, blocks[0], True), (1, blocks[1], True)]
    for j in range(nq - 1):
        in_first = j < p
        tiles.append((jnp.where(in_first, 0, 1), jnp.where(in_first, p - 1 - j, nq - 2 - j), False))
    keys = lambda kb: pl.ds(pl.multiple_of(kb * SB_BLOCK, SB_BLOCK), SB_BLOCK)

    def scores(sel, kb, diag):
        return _nt_dot(q2_ref[sel], k_ref[keys(kb), :]) * LOG2E

    def logs(sel, kb, diag, z2):
        log_beta = jnp.minimum(z2, 0.0) - jnp.log2(1.0 + jnp.exp2(-jnp.abs(z2)))
        log_1m = log_beta - z2
        if diag:
            log_1m = jnp.where(strict, log_1m, 0.0)
        hi = log_1m.astype(BF16)
        lo = (log_1m - hi.astype(F32)).astype(BF16)
        return log_beta, _dot(jnp.concatenate([hi, lo], axis=1), mo_ref[...])

    def weights(sel, kb, diag, log_beta, cs):
        after = cs[:, :SB_BLOCK]
        if diag:
            carry_ref[sel] = cs[:, SB_BLOCK:]
        else:
            after = after + carry_ref[sel]
            carry_ref[sel] += cs[:, SB_BLOCK:]
        a = jnp.exp2(log_beta + after)
        if diag:
            a = jnp.where(strict, a, 0.0)
        return _dot(a.astype(BF16), v_ref[keys(kb), :])

    def accumulate(sel, kb, diag, pv):
        if diag:
            acc_ref[sel] = pv
        else:
            acc_ref[sel] += pv

    stages = (scores, logs, weights, accumulate)
    live = {}
    for step in range(len(tiles) + len(stages) - 1):
        for depth, stage in enumerate(stages):
            t = step - depth
            if 0 <= t < len(tiles):
                prev = live.pop((t, depth - 1), ())
                out = stage(*tiles[t], *prev)
                live[(t, depth)] = out if isinstance(out, tuple) else (out,)
    for s, blk in enumerate(blocks):
        o_ref[pl.ds(pl.multiple_of(blk * SB_BLOCK, SB_BLOCK), SB_BLOCK), :] = jnp.where(
            first, acc_ref[s, :SB_BLOCK, :], acc_ref[s, SB_BLOCK:, :])


def _stick_breaking(sq, sk, sv, mo, batch, seq):
    t = sq.shape[0]
    nq = seq // SB_BLOCK
    pairs = SB_WIDTH // LANES
    spec = pl.BlockSpec((seq, LANES), lambda b, c, p: (b, c))
    pair_rows = pltpu.VMEM((2, 2 * SB_BLOCK, LANES), F32)
    return pl.pallas_call(
        _stick_breaking_kernel,
        grid=(batch, pairs, nq // 2),
        in_specs=[spec, spec, spec, pl.BlockSpec(mo.shape, lambda b, c, p: (0, 0))],
        out_specs=spec,
        out_shape=jax.ShapeDtypeStruct((t, SB_WIDTH), F32),
        scratch_shapes=[pltpu.VMEM((2, 2 * SB_BLOCK, LANES), BF16), pair_rows, pair_rows],
        compiler_params=_params("parallel", "parallel", "arbitrary"),
        name="stick_breaking",
    )(sq, sk, sv, mo)


def _cumsum_matrix():
    j = jnp.arange(SB_BLOCK)[:, None]
    s = jnp.arange(SB_BLOCK)[None, :]
    tri = (j > s).astype(BF16)
    half = jnp.concatenate([tri, jnp.ones((SB_BLOCK, SB_BLOCK), BF16)], axis=1)
    return jnp.concatenate([half, half], axis=0)


def _mem_kv_kernel(m_ref, g_ref, w_ref, gk_ref, k_ref, v_ref):
    d = m_ref.shape[1]
    hd = d // X_HEADS
    mn = _rms(m_ref[...], g_ref[...]).astype(BF16)
    kv = _dot(mn, w_ref[...])
    for h in range(X_HEADS):
        kh = kv[:, h * hd:(h + 1) * hd]
        k_ref[:, h * hd:(h + 1) * hd] = (_rms(kh, gk_ref[...]) * (hd ** -0.5)).astype(BF16)
    v_ref[...] = kv[:, d:].astype(BF16)


def _mem_kv(mem, g, w, gk):
    t, d = mem.shape
    row = lambda i: (i, 0)
    const = lambda i: (0, 0)
    out = pl.BlockSpec((ROW_TILE, d), row)
    return pl.pallas_call(
        _mem_kv_kernel,
        grid=(t // ROW_TILE,),
        in_specs=[pl.BlockSpec((ROW_TILE, d), row), pl.BlockSpec((1, d), const),
                  pl.BlockSpec(w.shape, const), pl.BlockSpec(gk.shape, const)],
        out_specs=[out, out],
        out_shape=[jax.ShapeDtypeStruct((t, d), BF16)] * 2,
        compiler_params=_params("parallel"),
        name="mem_kv",
    )(mem, g, w, gk)


def _mix_cross_kernel(x_ref, oret_ref, osb_ref, gsb_ref, wmix_ref, gc_ref, wxq_ref, gq_ref,
                      k_ref, v_ref, wxo_ref, o_ref):
    d = x_ref.shape[1]
    hd = d // X_HEADS
    nsb = _rms(osb_ref[...], gsb_ref[...]).astype(BF16)
    x = (x_ref[...] + _dot(oret_ref[...], wmix_ref[:RET_WIDTH, :])
         + _dot(nsb, wmix_ref[RET_WIDTH:, :]))
    q = _dot(_rms(x, gc_ref[...]).astype(BF16), wxq_ref[...])
    heads = []
    for h in range(X_HEADS):
        cols = slice(h * hd, (h + 1) * hd)
        qn = _rms(q[:, cols], gq_ref[...]).astype(BF16)
        s = _nt_dot(qn, k_ref[0, :, cols])
        e = jnp.exp(s - jnp.max(s, axis=-1, keepdims=True))
        p = (e / jnp.sum(e, axis=-1, keepdims=True)).astype(BF16)
        heads.append(_dot(p, v_ref[0, :, cols]).astype(BF16))
    o_ref[...] = x + _dot(jnp.concatenate(heads, axis=1), wxo_ref[...])


def _mix_cross(x, oret, osb, gsb, wmix, gc, wxq, gq, k, v, wxo, seq):
    t, d = x.shape
    per_batch = seq // ROW_TILE
    row = lambda i: (i, 0)
    const = lambda i: (0, 0)
    full = lambda a: pl.BlockSpec(a.shape, const)
    mem = pl.BlockSpec((1,) + k.shape[1:], lambda i: (i // per_batch, 0, 0))
    return pl.pallas_call(
        _mix_cross_kernel,
        grid=(t // ROW_TILE,),
        in_specs=[pl.BlockSpec((ROW_TILE, d), row), pl.BlockSpec((ROW_TILE, RET_WIDTH), row),
                  pl.BlockSpec((ROW_TILE, SB_WIDTH), row), full(gsb), full(wmix), full(gc),
                  full(wxq), full(gq), mem, mem, full(wxo)],
        out_specs=pl.BlockSpec((ROW_TILE, d), row),
        out_shape=jax.ShapeDtypeStruct((t, d), F32),
        compiler_params=_params("parallel"),
        name="mix_cross",
    )(x, oret, osb, gsb, wmix, gc, wxq, gq, k, v, wxo)


def _mlp_kernel(x_ref, g_ref, wup_ref, wdn_ref, o_ref, h_ref):
    x = x_ref[...]
    hm = _rms(x, g_ref[...]).astype(BF16)
    for f in range(h_ref.shape[1] // FF_CHUNK):
        cols = slice(f * FF_CHUNK, (f + 1) * FF_CHUNK)
        up = _dot(hm, wup_ref[:, cols])
        h_ref[:, cols] = jnp.square(jnp.maximum(up, 0.0)).astype(BF16)
    o_ref[...] = x + _dot(h_ref[...], wdn_ref[...])


def _mlp(x, g, wup, wdn):
    t, d = x.shape
    row = lambda i: (i, 0)
    const = lambda i: (0, 0)
    return pl.pallas_call(
        _mlp_kernel,
        grid=(t // ROW_TILE,),
        in_specs=[pl.BlockSpec((ROW_TILE, d), row), pl.BlockSpec((1, d), const),
                  pl.BlockSpec(wup.shape, const), pl.BlockSpec(wdn.shape, const)],
        out_specs=pl.BlockSpec((ROW_TILE, d), row),
        out_shape=jax.ShapeDtypeStruct((t, d), F32),
        scratch_shapes=[pltpu.VMEM((ROW_TILE, wup.shape[1]), BF16)],
        compiler_params=_params("parallel"),
        name="mlp",
    )(x, g, wup, wdn)


def _rotary_tables(seq):
    half = RET_HD // 2
    inv = 1.0 / (ROPE_BASE ** jnp.linspace(0.0, 1.0, half, dtype=F32))
    ang = jnp.arange(seq, dtype=F32)[:, None] * inv[None, :]
    cos, sin = jnp.cos(ang), jnp.sin(ang)
    return jnp.concatenate([cos, cos], axis=1), jnp.concatenate([-sin, sin], axis=1)


def kernel(x, mem, g_mix, w_in, g_ret_out, g_sb_out, w_mix_out, g_cross, g_mem,
           w_xq, w_xkv, g_qn, g_kn, w_xo, g_mlp, w_up, w_down):
    batch, seq, d = x.shape
    n_mem = mem.shape[1]
    depth = w_in.shape[0]
    cos, sin = _rotary_tables(seq)
    ret_tabs = _retention_tables()
    mo = _cumsum_matrix()
    xt = x.reshape(batch * seq, d)
    memt = mem.reshape(batch * n_mem, d)
    vec = lambda a: a.reshape(1, -1)
    for l in range(depth):
        rq, rk, rv, rg, sq, sk, sv = _in_proj(xt, vec(g_mix[l]), w_in[l].astype(BF16), cos, sin, seq)
        o_ret = _retention(rq, rk, rv, rg, ret_tabs, g_ret_out[l].reshape(RET_HEADS, 1, RET_HD),
                           batch, seq)
        o_sb = _stick_breaking(sq, sk, sv, mo, batch, seq)
        km, vm = _mem_kv(memt, vec(g_mem[l]), w_xkv[l].astype(BF16), vec(g_kn[l]))
        xt = _mix_cross(xt, o_ret, o_sb, vec(g_sb_out[l]), w_mix_out[l].astype(BF16),
                        vec(g_cross[l]), w_xq[l].astype(BF16), vec(g_qn[l]),
                        km.reshape(batch, n_mem, d), vm.reshape(batch, n_mem, d),
                        w_xo[l].astype(BF16), seq)
        xt = _mlp(xt, vec(g_mlp[l]), w_up[l].astype(BF16), w_down[l].astype(BF16))
    return xt.reshape(batch, seq, d)
```

```python
import functools
import math

import jax
import jax.numpy as jnp
from jax import lax
from jax.experimental import pallas as pl
from jax.experimental.pallas import tpu as pltpu

F32 = jnp.float32
BF16 = jnp.bfloat16

EPS = 1e-6
LOG2E = math.log2(math.e)
ROPE_BASE = 10000.0
CHUNK = 64
RET_HEADS = 4
RET_HD = 128
RET_WIDTH = RET_HEADS * RET_HD
SB_HEADS = 8
SB_HD = 64
SB_WIDTH = SB_HEADS * SB_HD
X_HEADS = 4
LANES = 128
ROW_TILE = 512
RET_BLOCK = 256
SB_BLOCK = 128
FF_CHUNK = 512
SB_STAGE_STEP = (0, 2, 4, 5)
SB_STAGE_ORDER = (0, 1, 2, 3)
VMEM_LIMIT = 56 * 1024 * 1024


def _nt_dot(a, b):
    return lax.dot_general(a, b, (((1,), (1,)), ((), ())), preferred_element_type=F32)


def _dot(a, b):
    return jnp.dot(a, b, preferred_element_type=F32)


def _rms(x, g):
    return x * lax.rsqrt(jnp.mean(x * x, axis=-1, keepdims=True) + EPS) * g


def _params(*sem):
    return pltpu.CompilerParams(dimension_semantics=sem, vmem_limit_bytes=VMEM_LIMIT)


def _in_proj_kernel(x_ref, g_ref, w_ref, cos_ref, sin_ref,
                    rq_ref, rk_ref, rv_ref, rg_ref, sq_ref, sk_ref, sv_ref):
    h = _rms(x_ref[...], g_ref[...]).astype(BF16)
    cos = cos_ref[...]
    sin = sin_ref[...]

    def proj(i):
        return _dot(h, w_ref[:, i * RET_WIDTH:(i + 1) * RET_WIDTH])

    def rotary(p, scale):
        for hd in range(RET_HEADS):
            c = p[:, hd * RET_HD:(hd + 1) * RET_HD]
            r = c * cos + pltpu.roll(c, RET_HD // 2, 1) * sin
            yield hd, r * scale if scale != 1.0 else r

    for hd, r in rotary(proj(0), 1.0):
        rq_ref[:, hd * RET_HD:(hd + 1) * RET_HD] = r.astype(BF16)
    for hd, r in rotary(proj(1), RET_HD ** -0.5):
        rk_ref[:, hd * RET_HD:(hd + 1) * RET_HD] = r.astype(BF16)
    rv_ref[...] = proj(2).astype(BF16)
    rg_ref[...] = proj(3)
    sq_ref[...] = (proj(4) * (SB_HD ** -0.5)).astype(BF16)
    sk_ref[...] = proj(5).astype(BF16)
    sv_ref[...] = proj(6).astype(BF16)


def _in_proj(x, g, w, cos, sin, seq):
    t, d = x.shape
    n_pos = seq // ROW_TILE
    row = lambda i: (i, 0)
    const = lambda i: (0, 0)
    half = pl.BlockSpec((ROW_TILE, RET_WIDTH), row)
    pos = pl.BlockSpec((ROW_TILE, RET_HD), lambda i: (i % n_pos, 0))
    outs = [jax.ShapeDtypeStruct((t, RET_WIDTH), dt)
            for dt in (BF16, BF16, BF16, F32, BF16, BF16, BF16)]
    return pl.pallas_call(
        _in_proj_kernel,
        grid=(t // ROW_TILE,),
        in_specs=[pl.BlockSpec((ROW_TILE, d), row), pl.BlockSpec((1, d), const),
                  pl.BlockSpec(w.shape, const), pos, pos],
        out_specs=[half] * 7,
        out_shape=outs,
        compiler_params=_params("parallel"),
        name="in_proj",
    )(x, g, w, cos, sin)


def _retention_kernel(q_ref, k_ref, v_ref, gate_ref, w_ref, qd_ref, kd_ref, cd_ref, gn_ref,
                      o_ref, state_ref):
    @pl.when(pl.program_id(1) == 0)
    def _():
        state_ref[...] = jnp.zeros_like(state_ref)

    for h in range(RET_HEADS):
        cols = slice(h * RET_HD, (h + 1) * RET_HD)
        q = q_ref[:, cols]
        k = k_ref[:, cols]
        v = v_ref[:, cols]
        state = state_ref[h]
        p = (_nt_dot(q, k) * w_ref[h]).astype(BF16)
        o = _dot(p, v) + _dot(q, state.astype(BF16)) * qd_ref[h]
        kd = (k.astype(F32) * kd_ref[h]).T.astype(BF16)
        state_ref[h] = state * cd_ref[h] + _dot(kd, v)
        gate = gate_ref[:, cols]
        o_ref[:, cols] = (_rms(o, gn_ref[h]) * (gate * jax.nn.sigmoid(gate))).astype(BF16)


def _retention(rq, rk, rv, rg, tabs, gn, batch, seq):
    t = rq.shape[0]
    nb = seq // RET_BLOCK
    tok = pl.BlockSpec((RET_BLOCK, RET_WIDTH), lambda b, i: (b * nb + i, 0))
    full = lambda a: pl.BlockSpec(a.shape, lambda b, i: (0,) * a.ndim)
    return pl.pallas_call(
        _retention_kernel,
        grid=(batch, nb),
        in_specs=[tok, tok, tok, tok] + [full(a) for a in (*tabs, gn)],
        out_specs=tok,
        out_shape=jax.ShapeDtypeStruct((t, RET_WIDTH), BF16),
        scratch_shapes=[pltpu.VMEM((RET_HEADS, RET_HD, RET_HD), F32)],
        compiler_params=_params("parallel", "arbitrary"),
        name="retention",
    )(rq, rk, rv, rg, *tabs, gn)


def _retention_tables():
    lg = jnp.log1p(-jnp.exp2(-5.0 - jnp.arange(RET_HEADS, dtype=F32)))[:, None, None]
    idx = jnp.arange(RET_BLOCK, dtype=F32)
    dist = jnp.abs(idx[:, None] - idx[None, :])
    chunk = jnp.arange(RET_BLOCK) // CHUNK
    visible = chunk[None, :] <= chunk[:, None]
    w = jnp.where(visible[None], jnp.exp(lg * dist[None]), 0.0)
    ones = jnp.ones((1, 1, RET_HD), F32)
    qd = jnp.exp(lg * (idx + 1.0)[None, :, None]) * ones
    kd = jnp.exp(lg * (RET_BLOCK - 1.0 - idx)[None, :, None]) * ones
    cd = jnp.exp(lg * RET_BLOCK) * ones
    return w, qd, kd, cd


def _stick_breaking_kernel(q_ref, k_ref, v_ref, mo_ref, o_ref, q2_ref):
    nq = q_ref.shape[0] // SB_BLOCK
    lane = lax.broadcasted_iota(jnp.int32, (SB_BLOCK, LANES), 1)
    first = lane < SB_HD
    rows = lambda blk: slice(blk * SB_BLOCK, (blk + 1) * SB_BLOCK)
    for blk in range(nq):
        q = q_ref[rows(blk), :].astype(F32)
        q2_ref[blk] = jnp.concatenate([jnp.where(first, q, 0.0), jnp.where(first, 0.0, q)],
                                      axis=0).astype(BF16)
    row = lax.broadcasted_iota(jnp.int32, (2 * SB_BLOCK, SB_BLOCK), 0) % SB_BLOCK
    col = lax.broadcasted_iota(jnp.int32, (2 * SB_BLOCK, SB_BLOCK), 1)
    strict = col < row

    tiles = [(blk, kb, kb == blk) for blk in range(nq) for kb in range(blk, -1, -1)]
    carry = {}
    acc = {}

    def scores(blk, kb, diag):
        return _nt_dot(q2_ref[blk], k_ref[rows(kb), :])

    def logs(blk, kb, diag, z):
        log_beta = jnp.minimum(z, 0.0) - jnp.log(1.0 + jnp.exp2(jnp.abs(z) * -LOG2E))
        log_1m = log_beta - z
        if diag:
            log_1m = jnp.where(strict, log_1m, 0.0)
        hi = log_1m.astype(BF16)
        lo = (log_1m - hi.astype(F32)).astype(BF16)
        return log_beta, _dot(jnp.concatenate([hi, lo], axis=1), mo_ref[...])

    def weights(blk, kb, diag, log_beta, cs):
        after = cs[:, :SB_BLOCK]
        if diag:
            carry[blk] = cs[:, SB_BLOCK:]
        else:
            after = after + carry[blk]
            carry[blk] = carry[blk] + cs[:, SB_BLOCK:] if kb else None
        a = jnp.exp2((log_beta + after) * LOG2E)
        if diag:
            a = jnp.where(strict, a, 0.0)
        return _dot(a.astype(BF16), v_ref[rows(kb), :])

    def accumulate(blk, kb, diag, pv):
        acc[blk] = pv if diag else acc[blk] + pv
        if kb == 0:
            out = acc.pop(blk)
            o_ref[rows(blk), :] = jnp.where(first, out[:SB_BLOCK, :], out[SB_BLOCK:, :])

    stages = (scores, logs, weights, accumulate)
    live = {}
    for step in range(len(tiles) + SB_STAGE_STEP[-1]):
        for depth in SB_STAGE_ORDER:
            t = step - SB_STAGE_STEP[depth]
            if 0 <= t < len(tiles):
                prev = live.pop((t, depth - 1), ())
                out = stages[depth](*tiles[t], *prev)
                live[(t, depth)] = out if isinstance(out, tuple) else (out,)


def _stick_breaking(sq, sk, sv, mo, batch, seq):
    t = sq.shape[0]
    nq = seq // SB_BLOCK
    pairs = SB_WIDTH // LANES
    spec = pl.BlockSpec((seq, LANES), lambda b, c: (b, c))
    return pl.pallas_call(
        _stick_breaking_kernel,
        grid=(batch, pairs),
        in_specs=[spec, spec, spec, pl.BlockSpec(mo.shape, lambda b, c: (0, 0))],
        out_specs=spec,
        out_shape=jax.ShapeDtypeStruct((t, SB_WIDTH), F32),
        scratch_shapes=[pltpu.VMEM((nq, 2 * SB_BLOCK, LANES), BF16)],
        compiler_params=_params("parallel", "parallel"),
        name="stick_breaking",
    )(sq, sk, sv, mo)


def _cumsum_matrix():
    j = jnp.arange(SB_BLOCK)[:, None]
    s = jnp.arange(SB_BLOCK)[None, :]
    tri = (j > s).astype(BF16)
    half = jnp.concatenate([tri, jnp.ones((SB_BLOCK, SB_BLOCK), BF16)], axis=1)
    return jnp.concatenate([half, half], axis=0)


def _mem_kv_kernel(m_ref, g_ref, w_ref, gk_ref, k_ref, v_ref):
    d = m_ref.shape[1]
    hd = d // X_HEADS
    mn = _rms(m_ref[...], g_ref[...]).astype(BF16)
    kv = _dot(mn, w_ref[...])
    for h in range(X_HEADS):
        kh = kv[:, h * hd:(h + 1) * hd]
        k_ref[:, h * hd:(h + 1) * hd] = (_rms(kh, gk_ref[...]) * (hd ** -0.5)).astype(BF16)
    v_ref[...] = kv[:, d:].astype(BF16)


def _mem_kv(mem, g, w, gk):
    t, d = mem.shape
    row = lambda i: (i, 0)
    const = lambda i: (0, 0)
    out = pl.BlockSpec((ROW_TILE, d), row)
    return pl.pallas_call(
        _mem_kv_kernel,
        grid=(t // ROW_TILE,),
        in_specs=[pl.BlockSpec((ROW_TILE, d), row), pl.BlockSpec((1, d), const),
                  pl.BlockSpec(w.shape, const), pl.BlockSpec(gk.shape, const)],
        out_specs=[out, out],
        out_shape=[jax.ShapeDtypeStruct((t, d), BF16)] * 2,
        compiler_params=_params("parallel"),
        name="mem_kv",
    )(mem, g, w, gk)


def _mix_cross_kernel(x_ref, oret_ref, osb_ref, gsb_ref, wmix_ref, gc_ref, wxq_ref, gq_ref,
                      k_ref, v_ref, wxo_ref, o_ref):
    d = x_ref.shape[1]
    hd = d // X_HEADS
    nsb = _rms(osb_ref[...], gsb_ref[...]).astype(BF16)
    x = (x_ref[...] + _dot(oret_ref[...], wmix_ref[:RET_WIDTH, :])
         + _dot(nsb, wmix_ref[RET_WIDTH:, :]))
    q = _dot(_rms(x, gc_ref[...]).astype(BF16), wxq_ref[...])
    heads = []
    for h in range(X_HEADS):
        cols = slice(h * hd, (h + 1) * hd)
        qn = _rms(q[:, cols], gq_ref[...]).astype(BF16)
        s = _nt_dot(qn, k_ref[0, :, cols])
        e = jnp.exp(s - jnp.max(s, axis=-1, keepdims=True))
        p = (e / jnp.sum(e, axis=-1, keepdims=True)).astype(BF16)
        heads.append(_dot(p, v_ref[0, :, cols]).astype(BF16))
    o_ref[...] = x + _dot(jnp.concatenate(heads, axis=1), wxo_ref[...])


def _mix_cross(x, oret, osb, gsb, wmix, gc, wxq, gq, k, v, wxo, seq):
    t, d = x.shape
    per_batch = seq // ROW_TILE
    row = lambda i: (i, 0)
    const = lambda i: (0, 0)
    full = lambda a: pl.BlockSpec(a.shape, const)
    mem = pl.BlockSpec((1,) + k.shape[1:], lambda i: (i // per_batch, 0, 0))
    return pl.pallas_call(
        _mix_cross_kernel,
        grid=(t // ROW_TILE,),
        in_specs=[pl.BlockSpec((ROW_TILE, d), row), pl.BlockSpec((ROW_TILE, RET_WIDTH), row),
                  pl.BlockSpec((ROW_TILE, SB_WIDTH), row), full(gsb), full(wmix), full(gc),
                  full(wxq), full(gq), mem, mem, full(wxo)],
        out_specs=pl.BlockSpec((ROW_TILE, d), row),
        out_shape=jax.ShapeDtypeStruct((t, d), F32),
        compiler_params=_params("parallel"),
        name="mix_cross",
    )(x, oret, osb, gsb, wmix, gc, wxq, gq, k, v, wxo)


def _mlp_kernel(x_ref, g_ref, wup_ref, wdn_ref, o_ref, h_ref):
    x = x_ref[...]
    hm = _rms(x, g_ref[...]).astype(BF16)
    for f in range(h_ref.shape[1] // FF_CHUNK):
        cols = slice(f * FF_CHUNK, (f + 1) * FF_CHUNK)
        up = _dot(hm, wup_ref[:, cols])
        h_ref[:, cols] = jnp.square(jnp.maximum(up, 0.0)).astype(BF16)
    o_ref[...] = x + _dot(h_ref[...], wdn_ref[...])


def _mlp(x, g, wup, wdn):
    t, d = x.shape
    row = lambda i: (i, 0)
    const = lambda i: (0, 0)
    return pl.pallas_call(
        _mlp_kernel,
        grid=(t // ROW_TILE,),
        in_specs=[pl.BlockSpec((ROW_TILE, d), row), pl.BlockSpec((1, d), const),
                  pl.BlockSpec(wup.shape, const), pl.BlockSpec(wdn.shape, const)],
        out_specs=pl.BlockSpec((ROW_TILE, d), row),
        out_shape=jax.ShapeDtypeStruct((t, d), F32),
        scratch_shapes=[pltpu.VMEM((ROW_TILE, wup.shape[1]), BF16)],
        compiler_params=_params("parallel"),
        name="mlp",
    )(x, g, wup, wdn)


def _rotary_tables(seq):
    half = RET_HD // 2
    inv = 1.0 / (ROPE_BASE ** jnp.linspace(0.0, 1.0, half, dtype=F32))
    ang = jnp.arange(seq, dtype=F32)[:, None] * inv[None, :]
    cos, sin = jnp.cos(ang), jnp.sin(ang)
    return jnp.concatenate([cos, cos], axis=1), jnp.concatenate([-sin, sin], axis=1)


def kernel(x, mem, g_mix, w_in, g_ret_out, g_sb_out, w_mix_out, g_cross, g_mem,
           w_xq, w_xkv, g_qn, g_kn, w_xo, g_mlp, w_up, w_down):
    batch, seq, d = x.shape
    n_mem = mem.shape[1]
    depth = w_in.shape[0]
    cos, sin = _rotary_tables(seq)
    ret_tabs = _retention_tables()
    mo = _cumsum_matrix()
    xt = x.reshape(batch * seq, d)
    memt = mem.reshape(batch * n_mem, d)
    vec = lambda a: a.reshape(1, -1)
    for l in range(depth):
        rq, rk, rv, rg, sq, sk, sv = _in_proj(xt, vec(g_mix[l]), w_in[l].astype(BF16), cos, sin, seq)
        o_ret = _retention(rq, rk, rv, rg, ret_tabs, g_ret_out[l].reshape(RET_HEADS, 1, RET_HD),
                           batch, seq)
        o_sb = _stick_breaking(sq, sk, sv, mo, batch, seq)
        km, vm = _mem_kv(memt, vec(g_mem[l]), w_xkv[l].astype(BF16), vec(g_kn[l]))
        xt = _mix_cross(xt, o_ret, o_sb, vec(g_sb_out[l]), w_mix_out[l].astype(BF16),
                        vec(g_cross[l]), w_xq[l].astype(BF16), vec(g_qn[l]),
                        km.reshape(batch, n_mem, d), vm.reshape(batch, n_mem, d),
                        w_xo[l].astype(BF16), seq)
        xt = _mlp(xt, vec(g_mlp[l]), w_up[l].astype(BF16), w_down[l].astype(BF16))
    return xt.reshape(batch, seq, d)
```

```python
import functools
import math

import jax
import jax.numpy as jnp
from jax import lax
from jax.experimental import pallas as pl
from jax.experimental.pallas import tpu as pltpu

F32 = jnp.float32
BF16 = jnp.bfloat16

EPS = 1e-6
LOG2E = math.log2(math.e)
ROPE_BASE = 10000.0
CHUNK = 64
RET_HEADS = 4
RET_HD = 128
RET_WIDTH = RET_HEADS * RET_HD
SB_HEADS = 8
SB_HD = 64
SB_WIDTH = SB_HEADS * SB_HD
X_HEADS = 4
LANES = 128
ROW_TILE = 512
RET_BLOCK = 256
SB_BLOCK = 128
FF_CHUNK = 512
SB_STAGE_STEP = (0, 2, 4, 5)
SB_STAGE_ORDER = (0, 1, 2, 3)
VMEM_LIMIT = 56 * 1024 * 1024


def _nt_dot(a, b):
    return lax.dot_general(a, b, (((1,), (1,)), ((), ())), preferred_element_type=F32)


def _dot(a, b):
    return jnp.dot(a, b, preferred_element_type=F32)


def _rms(x, g):
    return x * lax.rsqrt(jnp.mean(x * x, axis=-1, keepdims=True) + EPS) * g


def _params(*sem):
    return pltpu.CompilerParams(dimension_semantics=sem, vmem_limit_bytes=VMEM_LIMIT)


def _in_proj_kernel(x_ref, g_ref, w_ref, cos_ref, sin_ref,
                    rq_ref, rk_ref, rv_ref, rg_ref, sq_ref, sk_ref, sv_ref):
    h = _rms(x_ref[...], g_ref[...]).astype(BF16)
    cos = cos_ref[...]
    sin = sin_ref[...]

    def proj(i):
        return _dot(h, w_ref[:, i * RET_WIDTH:(i + 1) * RET_WIDTH])

    def rotary(p, scale):
        for hd in range(RET_HEADS):
            c = p[:, hd * RET_HD:(hd + 1) * RET_HD]
            r = c * cos + pltpu.roll(c, RET_HD // 2, 1) * sin
            yield hd, r * scale if scale != 1.0 else r

    for hd, r in rotary(proj(0), 1.0):
        rq_ref[:, hd * RET_HD:(hd + 1) * RET_HD] = r.astype(BF16)
    for hd, r in rotary(proj(1), RET_HD ** -0.5):
        rk_ref[:, hd * RET_HD:(hd + 1) * RET_HD] = r.astype(BF16)
    rv_ref[...] = proj(2).astype(BF16)
    rg_ref[...] = proj(3)
    sq_ref[...] = (proj(4) * (SB_HD ** -0.5)).astype(BF16)
    sk_ref[...] = proj(5).astype(BF16)
    sv_ref[...] = proj(6).astype(BF16)


def _in_proj(x, g, w, cos, sin, seq):
    t, d = x.shape
    n_pos = seq // ROW_TILE
    row = lambda i: (i, 0)
    const = lambda i: (0, 0)
    half = pl.BlockSpec((ROW_TILE, RET_WIDTH), row)
    pos = pl.BlockSpec((ROW_TILE, RET_HD), lambda i: (i % n_pos, 0))
    outs = [jax.ShapeDtypeStruct((t, RET_WIDTH), dt)
            for dt in (BF16, BF16, BF16, F32, BF16, BF16, BF16)]
    return pl.pallas_call(
        _in_proj_kernel,
        grid=(t // ROW_TILE,),
        in_specs=[pl.BlockSpec((ROW_TILE, d), row), pl.BlockSpec((1, d), const),
                  pl.BlockSpec(w.shape, const), pos, pos],
        out_specs=[half] * 7,
        out_shape=outs,
        compiler_params=_params("parallel"),
        name="in_proj",
    )(x, g, w, cos, sin)


def _retention_kernel(q_ref, k_ref, v_ref, gate_ref, w_ref, qd_ref, kd_ref, cd_ref, gn_ref,
                      o_ref, state_ref):
    @pl.when(pl.program_id(1) == 0)
    def _():
        state_ref[...] = jnp.zeros_like(state_ref)

    for h in range(RET_HEADS):
        cols = slice(h * RET_HD, (h + 1) * RET_HD)
        q = q_ref[:, cols]
        k = k_ref[:, cols]
        v = v_ref[:, cols]
        state = state_ref[h]
        p = (_nt_dot(q, k) * w_ref[h]).astype(BF16)
        o = _dot(p, v) + _dot(q, state.astype(BF16)) * qd_ref[h]
        kd = (k.astype(F32) * kd_ref[h]).T.astype(BF16)
        state_ref[h] = state * cd_ref[h] + _dot(kd, v)
        gate = gate_ref[:, cols]
        o_ref[:, cols] = (_rms(o, gn_ref[h]) * (gate * jax.nn.sigmoid(gate))).astype(BF16)


def _retention(rq, rk, rv, rg, tabs, gn, batch, seq):
    t = rq.shape[0]
    nb = seq // RET_BLOCK
    tok = pl.BlockSpec((RET_BLOCK, RET_WIDTH), lambda b, i: (b * nb + i, 0))
    full = lambda a: pl.BlockSpec(a.shape, lambda b, i: (0,) * a.ndim)
    return pl.pallas_call(
        _retention_kernel,
        grid=(batch, nb),
        in_specs=[tok, tok, tok, tok] + [full(a) for a in (*tabs, gn)],
        out_specs=tok,
        out_shape=jax.ShapeDtypeStruct((t, RET_WIDTH), BF16),
        scratch_shapes=[pltpu.VMEM((RET_HEADS, RET_HD, RET_HD), F32)],
        compiler_params=_params("parallel", "arbitrary"),
        name="retention",
    )(rq, rk, rv, rg, *tabs, gn)


def _retention_tables():
    lg = jnp.log1p(-jnp.exp2(-5.0 - jnp.arange(RET_HEADS, dtype=F32)))[:, None, None]
    idx = jnp.arange(RET_BLOCK, dtype=F32)
    dist = jnp.abs(idx[:, None] - idx[None, :])
    chunk = jnp.arange(RET_BLOCK) // CHUNK
    visible = chunk[None, :] <= chunk[:, None]
    w = jnp.where(visible[None], jnp.exp(lg * dist[None]), 0.0)
    ones = jnp.ones((1, 1, RET_HD), F32)
    qd = jnp.exp(lg * (idx + 1.0)[None, :, None]) * ones
    kd = jnp.exp(lg * (RET_BLOCK - 1.0 - idx)[None, :, None]) * ones
    cd = jnp.exp(lg * RET_BLOCK) * ones
    return w, qd, kd, cd


def _stick_breaking_kernel(q_ref, k_ref, v_ref, mo_ref, o_ref, q2_ref):
    nq = q_ref.shape[0] // SB_BLOCK
    lane = lax.broadcasted_iota(jnp.int32, (SB_BLOCK, LANES), 1)
    first = lane < SB_HD
    rows = lambda blk: slice(blk * SB_BLOCK, (blk + 1) * SB_BLOCK)
    for blk in range(nq):
        q = q_ref[rows(blk), :].astype(F32)
        q2_ref[blk] = jnp.concatenate([jnp.where(first, q, 0.0), jnp.where(first, 0.0, q)],
                                      axis=0).astype(BF16)
    row = lax.broadcasted_iota(jnp.int32, (2 * SB_BLOCK, SB_BLOCK), 0) % SB_BLOCK
    col = lax.broadcasted_iota(jnp.int32, (2 * SB_BLOCK, SB_BLOCK), 1)
    strict = col < row

    tiles = [(blk, kb, kb == blk) for blk in range(nq) for kb in range(blk, -1, -1)]
    carry = {}
    acc = {}

    def scores(blk, kb, diag):
        return _nt_dot(q2_ref[blk], k_ref[rows(kb), :])

    def logs(blk, kb, diag, z):
        log_beta = jnp.minimum(z, 0.0) - jnp.log(1.0 + jnp.exp2(jnp.abs(z) * -LOG2E))
        log_1m = log_beta - z
        if diag:
            log_1m = jnp.where(strict, log_1m, 0.0)
        if not diag:
            log_beta = log_beta + carry[blk]
        if kb:
            total = jnp.sum(log_1m, axis=-1, keepdims=True)
            carry[blk] = total if diag else carry[blk] + total
        return log_beta, _dot(log_1m.astype(BF16), mo_ref[...])

    def weights(blk, kb, diag, log_beta, after):
        a = jnp.exp2((log_beta + after) * LOG2E)
        if diag:
            a = jnp.where(strict, a, 0.0)
        return _dot(a.astype(BF16), v_ref[rows(kb), :])

    def accumulate(blk, kb, diag, pv):
        acc[blk] = pv if diag else acc[blk] + pv
        if kb == 0:
            out = acc.pop(blk)
            o_ref[rows(blk), :] = jnp.where(first, out[:SB_BLOCK, :], out[SB_BLOCK:, :])

    stages = (scores, logs, weights, accumulate)
    live = {}
    for step in range(len(tiles) + SB_STAGE_STEP[-1]):
        for depth in SB_STAGE_ORDER:
            t = step - SB_STAGE_STEP[depth]
            if 0 <= t < len(tiles):
                prev = live.pop((t, depth - 1), ())
                out = stages[depth](*tiles[t], *prev)
                live[(t, depth)] = out if isinstance(out, tuple) else (out,)


def _stick_breaking(sq, sk, sv, mo, batch, seq):
    t = sq.shape[0]
    nq = seq // SB_BLOCK
    pairs = SB_WIDTH // LANES
    spec = pl.BlockSpec((seq, LANES), lambda b, c: (b, c))
    return pl.pallas_call(
        _stick_breaking_kernel,
        grid=(batch, pairs),
        in_specs=[spec, spec, spec, pl.BlockSpec(mo.shape, lambda b, c: (0, 0))],
        out_specs=spec,
        out_shape=jax.ShapeDtypeStruct((t, SB_WIDTH), F32),
        scratch_shapes=[pltpu.VMEM((nq, 2 * SB_BLOCK, LANES), BF16)],
        compiler_params=_params("parallel", "parallel"),
        name="stick_breaking",
    )(sq, sk, sv, mo)


def _cumsum_matrix():
    j = jnp.arange(SB_BLOCK)[:, None]
    s = jnp.arange(SB_BLOCK)[None, :]
    return (j > s).astype(BF16)


def _mem_kv_kernel(m_ref, g_ref, w_ref, gk_ref, k_ref, v_ref):
    d = m_ref.shape[1]
    hd = d // X_HEADS
    mn = _rms(m_ref[...], g_ref[...]).astype(BF16)
    kv = _dot(mn, w_ref[...])
    for h in range(X_HEADS):
        kh = kv[:, h * hd:(h + 1) * hd]
        k_ref[:, h * hd:(h + 1) * hd] = (_rms(kh, gk_ref[...]) * (hd ** -0.5)).astype(BF16)
    v_ref[...] = kv[:, d:].astype(BF16)


def _mem_kv(mem, g, w, gk):
    t, d = mem.shape
    row = lambda i: (i, 0)
    const = lambda i: (0, 0)
    out = pl.BlockSpec((ROW_TILE, d), row)
    return pl.pallas_call(
        _mem_kv_kernel,
        grid=(t // ROW_TILE,),
        in_specs=[pl.BlockSpec((ROW_TILE, d), row), pl.BlockSpec((1, d), const),
                  pl.BlockSpec(w.shape, const), pl.BlockSpec(gk.shape, const)],
        out_specs=[out, out],
        out_shape=[jax.ShapeDtypeStruct((t, d), BF16)] * 2,
        compiler_params=_params("parallel"),
        name="mem_kv",
    )(mem, g, w, gk)


def _mix_cross_kernel(x_ref, oret_ref, osb_ref, gsb_ref, wmix_ref, gc_ref, wxq_ref, gq_ref,
                      k_ref, v_ref, wxo_ref, o_ref):
    d = x_ref.shape[1]
    hd = d // X_HEADS
    nsb = _rms(osb_ref[...], gsb_ref[...]).astype(BF16)
    x = (x_ref[...] + _dot(oret_ref[...], wmix_ref[:RET_WIDTH, :])
         + _dot(nsb, wmix_ref[RET_WIDTH:, :]))
    q = _dot(_rms(x, gc_ref[...]).astype(BF16), wxq_ref[...])
    heads = []
    for h in range(X_HEADS):
        cols = slice(h * hd, (h + 1) * hd)
        qn = _rms(q[:, cols], gq_ref[...]).astype(BF16)
        s = _nt_dot(qn, k_ref[0, :, cols])
        e = jnp.exp(s - jnp.max(s, axis=-1, keepdims=True))
        p = (e / jnp.sum(e, axis=-1, keepdims=True)).astype(BF16)
        heads.append(_dot(p, v_ref[0, :, cols]).astype(BF16))
    o_ref[...] = x + _dot(jnp.concatenate(heads, axis=1), wxo_ref[...])


def _mix_cross(x, oret, osb, gsb, wmix, gc, wxq, gq, k, v, wxo, seq):
    t, d = x.shape
    per_batch = seq // ROW_TILE
    row = lambda i: (i, 0)
    const = lambda i: (0, 0)
    full = lambda a: pl.BlockSpec(a.shape, const)
    mem = pl.BlockSpec((1,) + k.shape[1:], lambda i: (i // per_batch, 0, 0))
    return pl.pallas_call(
        _mix_cross_kernel,
        grid=(t // ROW_TILE,),
        in_specs=[pl.BlockSpec((ROW_TILE, d), row), pl.BlockSpec((ROW_TILE, RET_WIDTH), row),
                  pl.BlockSpec((ROW_TILE, SB_WIDTH), row), full(gsb), full(wmix), full(gc),
                  full(wxq), full(gq), mem, mem, full(wxo)],
        out_specs=pl.BlockSpec((ROW_TILE, d), row),
        out_shape=jax.ShapeDtypeStruct((t, d), F32),
        compiler_params=_params("parallel"),
        name="mix_cross",
    )(x, oret, osb, gsb, wmix, gc, wxq, gq, k, v, wxo)


def _mlp_kernel(x_ref, g_ref, wup_ref, wdn_ref, o_ref, h_ref):
    x = x_ref[...]
    hm = _rms(x, g_ref[...]).astype(BF16)
    for f in range(h_ref.shape[1] // FF_CHUNK):
        cols = slice(f * FF_CHUNK, (f + 1) * FF_CHUNK)
        up = _dot(hm, wup_ref[:, cols])
        h_ref[:, cols] = jnp.square(jnp.maximum(up, 0.0)).astype(BF16)
    o_ref[...] = x + _dot(h_ref[...], wdn_ref[...])


def _mlp(x, g, wup, wdn):
    t, d = x.shape
    row = lambda i: (i, 0)
    const = lambda i: (0, 0)
    return pl.pallas_call(
        _mlp_kernel,
        grid=(t // ROW_TILE,),
        in_specs=[pl.BlockSpec((ROW_TILE, d), row), pl.BlockSpec((1, d), const),
                  pl.BlockSpec(wup.shape, const), pl.BlockSpec(wdn.shape, const)],
        out_specs=pl.BlockSpec((ROW_TILE, d), row),
        out_shape=jax.ShapeDtypeStruct((t, d), F32),
        scratch_shapes=[pltpu.VMEM((ROW_TILE, wup.shape[1]), BF16)],
        compiler_params=_params("parallel"),
        name="mlp",
    )(x, g, wup, wdn)


def _rotary_tables(seq):
    half = RET_HD // 2
    inv = 1.0 / (ROPE_BASE ** jnp.linspace(0.0, 1.0, half, dtype=F32))
    ang = jnp.arange(seq, dtype=F32)[:, None] * inv[None, :]
    cos, sin = jnp.cos(ang), jnp.sin(ang)
    return jnp.concatenate([cos, cos], axis=1), jnp.concatenate([-sin, sin], axis=1)


def kernel(x, mem, g_mix, w_in, g_ret_out, g_sb_out, w_mix_out, g_cross, g_mem,
           w_xq, w_xkv, g_qn, g_kn, w_xo, g_mlp, w_up, w_down):
    batch, seq, d = x.shape
    n_mem = mem.shape[1]
    depth = w_in.shape[0]
    cos, sin = _rotary_tables(seq)
    ret_tabs = _retention_tables()
    mo = _cumsum_matrix()
    xt = x.reshape(batch * seq, d)
    memt = mem.reshape(batch * n_mem, d)
    vec = lambda a: a.reshape(1, -1)
    for l in range(depth):
        rq, rk, rv, rg, sq, sk, sv = _in_proj(xt, vec(g_mix[l]), w_in[l].astype(BF16), cos, sin, seq)
        o_ret = _retention(rq, rk, rv, rg, ret_tabs, g_ret_out[l].reshape(RET_HEADS, 1, RET_HD),
                           batch, seq)
        o_sb = _stick_breaking(sq, sk, sv, mo, batch, seq)
        km, vm = _mem_kv(memt, vec(g_mem[l]), w_xkv[l].astype(BF16), vec(g_kn[l]))
        xt = _mix_cross(xt, o_ret, o_sb, vec(g_sb_out[l]), w_mix_out[l].astype(BF16),
                        vec(g_cross[l]), w_xq[l].astype(BF16), vec(g_qn[l]),
                        km.reshape(batch, n_mem, d), vm.reshape(batch, n_mem, d),
                        w_xo[l].astype(BF16), seq)
        xt = _mlp(xt, vec(g_mlp[l]), w_up[l].astype(BF16), w_down[l].astype(BF16))
    return xt.reshape(batch, seq, d)
```

```python
import functools
import math

import jax
import jax.numpy as jnp
from jax import lax
from jax.experimental import pallas as pl
from jax.experimental.pallas import tpu as pltpu

F32 = jnp.float32
BF16 = jnp.bfloat16

EPS = 1e-6
LOG2E = math.log2(math.e)
ROPE_BASE = 10000.0
CHUNK = 64
RET_HEADS = 4
RET_HD = 128
RET_WIDTH = RET_HEADS * RET_HD
SB_HEADS = 8
SB_HD = 64
SB_WIDTH = SB_HEADS * SB_HD
X_HEADS = 4
LANES = 128
ROW_TILE = 1024
MLP_TILE = 512
RET_BLOCK = 256
SB_BLOCK = 128
FF_CHUNK = 512
SB_STAGE_STEP = (0, 2, 4, 5)
SB_STAGE_ORDER = (0, 1, 2, 3)
VMEM_LIMIT = 56 * 1024 * 1024


def _nt_dot(a, b):
    return lax.dot_general(a, b, (((1,), (1,)), ((), ())), preferred_element_type=F32)


def _dot(a, b):
    return jnp.dot(a, b, preferred_element_type=F32)


def _rms(x, g):
    return x * lax.rsqrt(jnp.mean(x * x, axis=-1, keepdims=True) + EPS) * g


def _params(*sem):
    return pltpu.CompilerParams(dimension_semantics=sem, vmem_limit_bytes=VMEM_LIMIT)


def _resident(a):
    return pl.BlockSpec(a.shape, lambda *_: (0,) * a.ndim, pipeline_mode=pl.Buffered(1))


def _in_proj_kernel(x_ref, g_ref, w_ref, cos_ref, sin_ref,
                    rq_ref, rk_ref, rv_ref, rg_ref, sq_ref, sk_ref, sv_ref):
    h = _rms(x_ref[...], g_ref[...]).astype(BF16)
    cos = cos_ref[...]
    sin = sin_ref[...]

    def proj(i):
        return _dot(h, w_ref[:, i * RET_WIDTH:(i + 1) * RET_WIDTH])

    def rotary(p, scale):
        for hd in range(RET_HEADS):
            c = p[:, hd * RET_HD:(hd + 1) * RET_HD]
            r = c * cos + pltpu.roll(c, RET_HD // 2, 1) * sin
            yield hd, r * scale if scale != 1.0 else r

    for hd, r in rotary(proj(0), 1.0):
        rq_ref[:, hd * RET_HD:(hd + 1) * RET_HD] = r.astype(BF16)
    for hd, r in rotary(proj(1), RET_HD ** -0.5):
        rk_ref[:, hd * RET_HD:(hd + 1) * RET_HD] = r.astype(BF16)
    rv_ref[...] = proj(2).astype(BF16)
    rg_ref[...] = proj(3)
    sq_ref[...] = (proj(4) * (SB_HD ** -0.5)).astype(BF16)
    sk_ref[...] = proj(5).astype(BF16)
    sv_ref[...] = proj(6).astype(BF16)


def _in_proj(x, g, w, cos, sin, seq):
    t, d = x.shape
    n_pos = seq // ROW_TILE
    row = lambda i: (i, 0)
    half = pl.BlockSpec((ROW_TILE, RET_WIDTH), row)
    pos = pl.BlockSpec((ROW_TILE, RET_HD), lambda i: (i % n_pos, 0))
    outs = [jax.ShapeDtypeStruct((t, RET_WIDTH), dt)
            for dt in (BF16, BF16, BF16, F32, BF16, BF16, BF16)]
    return pl.pallas_call(
        _in_proj_kernel,
        grid=(t // ROW_TILE,),
        in_specs=[pl.BlockSpec((ROW_TILE, d), row), _resident(g), _resident(w), pos, pos],
        out_specs=[half] * 7,
        out_shape=outs,
        compiler_params=_params("parallel"),
        name="in_proj",
    )(x, g, w, cos, sin)


def _retention_kernel(q_ref, k_ref, v_ref, gate_ref, w_ref, qd_ref, kd_ref, cd_ref, gn_ref, o_ref):
    state = [None] * RET_HEADS
    nb = q_ref.shape[0] // RET_BLOCK
    for blk in range(nb):
        rows = slice(blk * RET_BLOCK, (blk + 1) * RET_BLOCK)
        for h in range(RET_HEADS):
            cols = slice(h * RET_HD, (h + 1) * RET_HD)
            q = q_ref[rows, cols]
            k = k_ref[rows, cols]
            v = v_ref[rows, cols]
            p = (_nt_dot(q, k) * w_ref[h]).astype(BF16)
            o = _dot(p, v)
            if blk:
                o = o + _dot(q, state[h].astype(BF16)) * qd_ref[h]
            if blk < nb - 1:
                kd = (k.astype(F32) * kd_ref[h]).T.astype(BF16)
                update = _dot(kd, v)
                state[h] = state[h] * cd_ref[h] + update if blk else update
            gate = gate_ref[rows, cols]
            o_ref[rows, cols] = (_rms(o, gn_ref[h]) * (gate * jax.nn.sigmoid(gate))).astype(BF16)


def _retention(rq, rk, rv, rg, tabs, gn, batch, seq):
    t = rq.shape[0]
    tok = pl.BlockSpec((seq, RET_WIDTH), lambda b: (b, 0))
    return pl.pallas_call(
        _retention_kernel,
        grid=(batch,),
        in_specs=[tok, tok, tok, tok] + [_resident(a) for a in (*tabs, gn)],
        out_specs=tok,
        out_shape=jax.ShapeDtypeStruct((t, RET_WIDTH), BF16),
        compiler_params=_params("parallel"),
        name="retention",
    )(rq, rk, rv, rg, *tabs, gn)


def _retention_tables():
    lg = jnp.log1p(-jnp.exp2(-5.0 - jnp.arange(RET_HEADS, dtype=F32)))[:, None, None]
    idx = jnp.arange(RET_BLOCK, dtype=F32)
    dist = jnp.abs(idx[:, None] - idx[None, :])
    chunk = jnp.arange(RET_BLOCK) // CHUNK
    visible = chunk[None, :] <= chunk[:, None]
    w = jnp.where(visible[None], jnp.exp(lg * dist[None]), 0.0)
    ones = jnp.ones((1, 1, RET_HD), F32)
    qd = jnp.exp(lg * (idx + 1.0)[None, :, None]) * ones
    kd = jnp.exp(lg * (RET_BLOCK - 1.0 - idx)[None, :, None]) * ones
    cd = jnp.exp(lg * RET_BLOCK) * ones
    return w, qd, kd, cd


def _stick_breaking_kernel(q_ref, k_ref, v_ref, mo_ref, o_ref, q2_ref):
    nq = q_ref.shape[0] // SB_BLOCK
    lane = lax.broadcasted_iota(jnp.int32, (SB_BLOCK, LANES), 1)
    first = lane < SB_HD
    rows = lambda blk: slice(blk * SB_BLOCK, (blk + 1) * SB_BLOCK)
    for blk in range(nq):
        q = q_ref[rows(blk), :].astype(F32)
        q2_ref[blk] = jnp.concatenate([jnp.where(first, q, 0.0), jnp.where(first, 0.0, q)],
                                      axis=0).astype(BF16)
    row = lax.broadcasted_iota(jnp.int32, (2 * SB_BLOCK, SB_BLOCK), 0) % SB_BLOCK
    col = lax.broadcasted_iota(jnp.int32, (2 * SB_BLOCK, SB_BLOCK), 1)
    strict = col < row

    tiles = [(blk, kb, kb == blk) for blk in range(nq) for kb in range(blk, -1, -1)]
    carry = {}
    acc = {}

    def scores(blk, kb, diag):
        return _nt_dot(q2_ref[blk], k_ref[rows(kb), :])

    def logs(blk, kb, diag, z):
        log_beta = jnp.minimum(z, 0.0) - jnp.log(1.0 + jnp.exp2(jnp.abs(z) * -LOG2E))
        log_1m = log_beta - z
        if diag:
            log_1m = jnp.where(strict, log_1m, 0.0)
        if not diag:
            log_beta = log_beta + carry[blk]
        if kb:
            total = jnp.sum(log_1m, axis=-1, keepdims=True)
            carry[blk] = total if diag else carry[blk] + total
        return log_beta, _dot(log_1m.astype(BF16), mo_ref[...])

    def weights(blk, kb, diag, log_beta, after):
        a = jnp.exp2((log_beta + after) * LOG2E)
        if diag:
            a = jnp.where(strict, a, 0.0)
        return _dot(a.astype(BF16), v_ref[rows(kb), :])

    def accumulate(blk, kb, diag, pv):
        acc[blk] = pv if diag else acc[blk] + pv
        if kb == 0:
            out = acc.pop(blk)
            o_ref[rows(blk), :] = jnp.where(first, out[:SB_BLOCK, :], out[SB_BLOCK:, :])

    stages = (scores, logs, weights, accumulate)
    live = {}
    for step in range(len(tiles) + SB_STAGE_STEP[-1]):
        for depth in SB_STAGE_ORDER:
            t = step - SB_STAGE_STEP[depth]
            if 0 <= t < len(tiles):
                prev = live.pop((t, depth - 1), ())
                out = stages[depth](*tiles[t], *prev)
                live[(t, depth)] = out if isinstance(out, tuple) else (out,)


def _stick_breaking(sq, sk, sv, mo, batch, seq):
    t = sq.shape[0]
    nq = seq // SB_BLOCK
    pairs = SB_WIDTH // LANES
    spec = pl.BlockSpec((seq, LANES), lambda b, c: (b, c))
    return pl.pallas_call(
        _stick_breaking_kernel,
        grid=(batch, pairs),
        in_specs=[spec, spec, spec, pl.BlockSpec(mo.shape, lambda b, c: (0, 0))],
        out_specs=spec,
        out_shape=jax.ShapeDtypeStruct((t, SB_WIDTH), F32),
        scratch_shapes=[pltpu.VMEM((nq, 2 * SB_BLOCK, LANES), BF16)],
        compiler_params=_params("parallel", "parallel"),
        name="stick_breaking",
    )(sq, sk, sv, mo)


def _cumsum_matrix():
    j = jnp.arange(SB_BLOCK)[:, None]
    s = jnp.arange(SB_BLOCK)[None, :]
    return (j > s).astype(BF16)


def _mem_kv_kernel(m_ref, g_ref, w_ref, gk_ref, k_ref, v_ref):
    d = m_ref.shape[1]
    hd = d // X_HEADS
    mn = _rms(m_ref[...], g_ref[...]).astype(BF16)
    kv = _dot(mn, w_ref[...])
    for h in range(X_HEADS):
        kh = kv[:, h * hd:(h + 1) * hd]
        k_ref[:, h * hd:(h + 1) * hd] = (_rms(kh, gk_ref[...]) * (hd ** -0.5)).astype(BF16)
    v_ref[...] = kv[:, d:].astype(BF16)


def _mem_kv(mem, g, w, gk):
    t, d = mem.shape
    row = lambda i: (i, 0)
    out = pl.BlockSpec((ROW_TILE, d), row)
    return pl.pallas_call(
        _mem_kv_kernel,
        grid=(t // ROW_TILE,),
        in_specs=[pl.BlockSpec((ROW_TILE, d), row), _resident(g), _resident(w), _resident(gk)],
        out_specs=[out, out],
        out_shape=[jax.ShapeDtypeStruct((t, d), BF16)] * 2,
        compiler_params=_params("parallel"),
        name="mem_kv",
    )(mem, g, w, gk)


def _mix_cross_kernel(x_ref, oret_ref, osb_ref, gsb_ref, wmix_ref, gc_ref, wxq_ref, gq_ref,
                      k_ref, v_ref, wxo_ref, o_ref):
    d = x_ref.shape[1]
    hd = d // X_HEADS
    nsb = _rms(osb_ref[...], gsb_ref[...]).astype(BF16)
    x = (x_ref[...] + _dot(oret_ref[...], wmix_ref[:RET_WIDTH, :])
         + _dot(nsb, wmix_ref[RET_WIDTH:, :]))
    q = _dot(_rms(x, gc_ref[...]).astype(BF16), wxq_ref[...])
    heads = []
    for h in range(X_HEADS):
        cols = slice(h * hd, (h + 1) * hd)
        qn = _rms(q[:, cols], gq_ref[...]).astype(BF16)
        s = _nt_dot(qn, k_ref[0, :, cols])
        e = jnp.exp(s - jnp.max(s, axis=-1, keepdims=True))
        p = (e / jnp.sum(e, axis=-1, keepdims=True)).astype(BF16)
        heads.append(_dot(p, v_ref[0, :, cols]).astype(BF16))
    o_ref[...] = x + _dot(jnp.concatenate(heads, axis=1), wxo_ref[...])


def _mix_cross(x, oret, osb, gsb, wmix, gc, wxq, gq, k, v, wxo, seq):
    t, d = x.shape
    per_batch = seq // ROW_TILE
    row = lambda i: (i, 0)
    full = _resident
    mem = pl.BlockSpec((1,) + k.shape[1:], lambda i: (i // per_batch, 0, 0))
    return pl.pallas_call(
        _mix_cross_kernel,
        grid=(t // ROW_TILE,),
        in_specs=[pl.BlockSpec((ROW_TILE, d), row), pl.BlockSpec((ROW_TILE, RET_WIDTH), row),
                  pl.BlockSpec((ROW_TILE, SB_WIDTH), row), full(gsb), full(wmix), full(gc),
                  full(wxq), full(gq), mem, mem, full(wxo)],
        out_specs=pl.BlockSpec((ROW_TILE, d), row),
        out_shape=jax.ShapeDtypeStruct((t, d), F32),
        compiler_params=_params("parallel"),
        name="mix_cross",
    )(x, oret, osb, gsb, wmix, gc, wxq, gq, k, v, wxo)


def _mlp_kernel(x_ref, g_ref, wup_ref, wdn_ref, o_ref, h_ref):
    x = x_ref[...]
    hm = _rms(x, g_ref[...]).astype(BF16)
    for f in range(h_ref.shape[1] // FF_CHUNK):
        cols = slice(f * FF_CHUNK, (f + 1) * FF_CHUNK)
        up = _dot(hm, wup_ref[:, cols])
        h_ref[:, cols] = jnp.square(jnp.maximum(up, 0.0)).astype(BF16)
    o_ref[...] = x + _dot(h_ref[...], wdn_ref[...])


def _mlp(x, g, wup, wdn):
    t, d = x.shape
    row = lambda i: (i, 0)
    return pl.pallas_call(
        _mlp_kernel,
        grid=(t // MLP_TILE,),
        in_specs=[pl.BlockSpec((MLP_TILE, d), row), _resident(g), _resident(wup), _resident(wdn)],
        out_specs=pl.BlockSpec((MLP_TILE, d), row),
        out_shape=jax.ShapeDtypeStruct((t, d), F32),
        scratch_shapes=[pltpu.VMEM((MLP_TILE, wup.shape[1]), BF16)],
        compiler_params=_params("parallel"),
        name="mlp",
    )(x, g, wup, wdn)


def _rotary_tables(seq):
    half = RET_HD // 2
    inv = 1.0 / (ROPE_BASE ** jnp.linspace(0.0, 1.0, half, dtype=F32))
    ang = jnp.arange(seq, dtype=F32)[:, None] * inv[None, :]
    cos, sin = jnp.cos(ang), jnp.sin(ang)
    return jnp.concatenate([cos, cos], axis=1), jnp.concatenate([-sin, sin], axis=1)


def kernel(x, mem, g_mix, w_in, g_ret_out, g_sb_out, w_mix_out, g_cross, g_mem,
           w_xq, w_xkv, g_qn, g_kn, w_xo, g_mlp, w_up, w_down):
    batch, seq, d = x.shape
    n_mem = mem.shape[1]
    depth = w_in.shape[0]
    cos, sin = _rotary_tables(seq)
    ret_tabs = _retention_tables()
    mo = _cumsum_matrix()
    xt = x.reshape(batch * seq, d)
    memt = mem.reshape(batch * n_mem, d)
    vec = lambda a: a.reshape(1, -1)
    for l in range(depth):
        rq, rk, rv, rg, sq, sk, sv = _in_proj(xt, vec(g_mix[l]), w_in[l].astype(BF16), cos, sin, seq)
        o_ret = _retention(rq, rk, rv, rg, ret_tabs, g_ret_out[l].reshape(RET_HEADS, 1, RET_HD),
                           batch, seq)
        o_sb = _stick_breaking(sq, sk, sv, mo, batch, seq)
        km, vm = _mem_kv(memt, vec(g_mem[l]), w_xkv[l].astype(BF16), vec(g_kn[l]))
        xt = _mix_cross(xt, o_ret, o_sb, vec(g_sb_out[l]), w_mix_out[l].astype(BF16),
                        vec(g_cross[l]), w_xq[l].astype(BF16), vec(g_qn[l]),
                        km.reshape(batch, n_mem, d), vm.reshape(batch, n_mem, d),
                        w_xo[l].astype(BF16), seq)
        xt = _mlp(xt, vec(g_mlp[l]), w_up[l].astype(BF16), w_down[l].astype(BF16))
    return xt.reshape(batch, seq, d)
```

```python
import functools
import math

import jax
import jax.numpy as jnp
from jax import lax
from jax.experimental import pallas as pl
from jax.experimental.pallas import tpu as pltpu

F32 = jnp.float32
BF16 = jnp.bfloat16

EPS = 1e-6
LOG2E = math.log2(math.e)
ROPE_BASE = 10000.0
CHUNK = 64
RET_HEADS = 4
RET_HD = 128
RET_WIDTH = RET_HEADS * RET_HD
SB_HEADS = 8
SB_HD = 64
SB_WIDTH = SB_HEADS * SB_HD
X_HEADS = 4
LANES = 128
ROW_TILE = 1024
MLP_TILE = 512
RET_BLOCK = 256
SB_BLOCK = 128
FF_CHUNK = 512
SB_STAGE_STEP = (0, 2, 4, 5)
SB_STAGE_ORDER = (0, 1, 2, 3)
VMEM_LIMIT = 56 * 1024 * 1024


def _nt_dot(a, b):
    return lax.dot_general(a, b, (((1,), (1,)), ((), ())), preferred_element_type=F32)


def _dot(a, b):
    return jnp.dot(a, b, preferred_element_type=F32)


def _rms(x, g):
    return x * lax.rsqrt(jnp.mean(x * x, axis=-1, keepdims=True) + EPS) * g


def _params(*sem):
    return pltpu.CompilerParams(dimension_semantics=sem, vmem_limit_bytes=VMEM_LIMIT)


def _resident(a):
    return pl.BlockSpec(a.shape, lambda *_: (0,) * a.ndim, pipeline_mode=pl.Buffered(1))


def _layer(a, l):
    return pl.BlockSpec((None,) + a.shape[1:], lambda *_: (l,) + (0,) * (a.ndim - 1),
                        pipeline_mode=pl.Buffered(1))


def _bf16(w):
    return w.astype(BF16)


def _in_proj_kernel(x_ref, g_ref, w_ref, cos_ref, sin_ref,
                    rq_ref, rk_ref, rv_ref, rg_ref, sq_ref, sk_ref, sv_ref):
    h = _rms(x_ref[...], g_ref[...]).astype(BF16)
    cos = cos_ref[...]
    sin = sin_ref[...]

    def proj(i):
        return _dot(h, _bf16(w_ref[:, i * RET_WIDTH:(i + 1) * RET_WIDTH]))

    def rotary(p, scale):
        for hd in range(RET_HEADS):
            c = p[:, hd * RET_HD:(hd + 1) * RET_HD]
            r = c * cos + pltpu.roll(c, RET_HD // 2, 1) * sin
            yield hd, r * scale if scale != 1.0 else r

    for hd, r in rotary(proj(0), 1.0):
        rq_ref[:, hd * RET_HD:(hd + 1) * RET_HD] = r.astype(BF16)
    for hd, r in rotary(proj(1), RET_HD ** -0.5):
        rk_ref[:, hd * RET_HD:(hd + 1) * RET_HD] = r.astype(BF16)
    rv_ref[...] = proj(2).astype(BF16)
    rg_ref[...] = proj(3)
    sq_ref[...] = (proj(4) * (SB_HD ** -0.5)).astype(BF16)
    sk_ref[...] = proj(5).astype(BF16)
    sv_ref[...] = proj(6).astype(BF16)


def _in_proj(x, g, w, l, cos, sin, seq):
    t, d = x.shape
    n_pos = seq // ROW_TILE
    row = lambda i: (i, 0)
    half = pl.BlockSpec((ROW_TILE, RET_WIDTH), row)
    pos = pl.BlockSpec((ROW_TILE, RET_HD), lambda i: (i % n_pos, 0))
    outs = [jax.ShapeDtypeStruct((t, RET_WIDTH), dt)
            for dt in (BF16, BF16, BF16, F32, BF16, BF16, BF16)]
    return pl.pallas_call(
        _in_proj_kernel,
        grid=(t // ROW_TILE,),
        in_specs=[pl.BlockSpec((ROW_TILE, d), row), _layer(g, l), _layer(w, l), pos, pos],
        out_specs=[half] * 7,
        out_shape=outs,
        compiler_params=_params("parallel"),
        name="in_proj",
    )(x, g, w, cos, sin)


def _retention_kernel(q_ref, k_ref, v_ref, gate_ref, w_ref, qd_ref, kd_ref, cd_ref, gn_ref, o_ref):
    state = [None] * RET_HEADS
    nb = q_ref.shape[0] // RET_BLOCK
    for blk in range(nb):
        rows = slice(blk * RET_BLOCK, (blk + 1) * RET_BLOCK)
        for h in range(RET_HEADS):
            cols = slice(h * RET_HD, (h + 1) * RET_HD)
            q = q_ref[rows, cols]
            k = k_ref[rows, cols]
            v = v_ref[rows, cols]
            p = (_nt_dot(q, k) * w_ref[h]).astype(BF16)
            o = _dot(p, v)
            if blk:
                o = o + _dot(q, state[h].astype(BF16)) * qd_ref[h]
            if blk < nb - 1:
                kd = (k.astype(F32) * kd_ref[h]).T.astype(BF16)
                update = _dot(kd, v)
                state[h] = state[h] * cd_ref[h] + update if blk else update
            gate = gate_ref[rows, cols]
            o_ref[rows, cols] = (_rms(o, gn_ref[h]) * (gate * jax.nn.sigmoid(gate))).astype(BF16)


def _retention(rq, rk, rv, rg, tabs, gn, batch, seq):
    t = rq.shape[0]
    tok = pl.BlockSpec((seq, RET_WIDTH), lambda b: (b, 0))
    return pl.pallas_call(
        _retention_kernel,
        grid=(batch,),
        in_specs=[tok, tok, tok, tok] + [_resident(a) for a in (*tabs, gn)],
        out_specs=tok,
        out_shape=jax.ShapeDtypeStruct((t, RET_WIDTH), BF16),
        compiler_params=_params("parallel"),
        name="retention",
    )(rq, rk, rv, rg, *tabs, gn)


def _retention_tables():
    lg = jnp.log1p(-jnp.exp2(-5.0 - jnp.arange(RET_HEADS, dtype=F32)))[:, None, None]
    idx = jnp.arange(RET_BLOCK, dtype=F32)
    dist = jnp.abs(idx[:, None] - idx[None, :])
    chunk = jnp.arange(RET_BLOCK) // CHUNK
    visible = chunk[None, :] <= chunk[:, None]
    w = jnp.where(visible[None], jnp.exp(lg * dist[None]), 0.0)
    ones = jnp.ones((1, 1, RET_HD), F32)
    qd = jnp.exp(lg * (idx + 1.0)[None, :, None]) * ones
    kd = jnp.exp(lg * (RET_BLOCK - 1.0 - idx)[None, :, None]) * ones
    cd = jnp.exp(lg * RET_BLOCK) * ones
    return w, qd, kd, cd


def _stick_breaking_kernel(q_ref, k_ref, v_ref, mo_ref, o_ref, q2_ref):
    nq = q_ref.shape[0] // SB_BLOCK
    lane = lax.broadcasted_iota(jnp.int32, (SB_BLOCK, LANES), 1)
    first = lane < SB_HD
    rows = lambda blk: slice(blk * SB_BLOCK, (blk + 1) * SB_BLOCK)
    for blk in range(nq):
        q = q_ref[rows(blk), :].astype(F32)
        q2_ref[blk] = jnp.concatenate([jnp.where(first, q, 0.0), jnp.where(first, 0.0, q)],
                                      axis=0).astype(BF16)
    row = lax.broadcasted_iota(jnp.int32, (2 * SB_BLOCK, SB_BLOCK), 0) % SB_BLOCK
    col = lax.broadcasted_iota(jnp.int32, (2 * SB_BLOCK, SB_BLOCK), 1)
    strict = col < row

    tiles = [(blk, kb, kb == blk) for blk in range(nq) for kb in range(blk, -1, -1)]
    carry = {}
    acc = {}

    def scores(blk, kb, diag):
        return _nt_dot(q2_ref[blk], k_ref[rows(kb), :])

    def logs(blk, kb, diag, z):
        log_beta = jnp.minimum(z, 0.0) - jnp.log(1.0 + jnp.exp2(jnp.abs(z) * -LOG2E))
        log_1m = log_beta - z
        if diag:
            log_1m = jnp.where(strict, log_1m, 0.0)
        if not diag:
            log_beta = log_beta + carry[blk]
        if kb:
            total = jnp.sum(log_1m, axis=-1, keepdims=True)
            carry[blk] = total if diag else carry[blk] + total
        return log_beta, _dot(log_1m.astype(BF16), mo_ref[...])

    def weights(blk, kb, diag, log_beta, after):
        a = jnp.exp2((log_beta + after) * LOG2E)
        if diag:
            a = jnp.where(strict, a, 0.0)
        return _dot(a.astype(BF16), v_ref[rows(kb), :])

    def accumulate(blk, kb, diag, pv):
        acc[blk] = pv if diag else acc[blk] + pv
        if kb == 0:
            out = acc.pop(blk)
            o_ref[rows(blk), :] = jnp.where(first, out[:SB_BLOCK, :], out[SB_BLOCK:, :])

    stages = (scores, logs, weights, accumulate)
    live = {}
    for step in range(len(tiles) + SB_STAGE_STEP[-1]):
        for depth in SB_STAGE_ORDER:
            t = step - SB_STAGE_STEP[depth]
            if 0 <= t < len(tiles):
                prev = live.pop((t, depth - 1), ())
                out = stages[depth](*tiles[t], *prev)
                live[(t, depth)] = out if isinstance(out, tuple) else (out,)


def _stick_breaking(sq, sk, sv, mo, batch, seq):
    t = sq.shape[0]
    nq = seq // SB_BLOCK
    pairs = SB_WIDTH // LANES
    spec = pl.BlockSpec((seq, LANES), lambda b, c: (b, c))
    return pl.pallas_call(
        _stick_breaking_kernel,
        grid=(batch, pairs),
        in_specs=[spec, spec, spec, pl.BlockSpec(mo.shape, lambda b, c: (0, 0))],
        out_specs=spec,
        out_shape=jax.ShapeDtypeStruct((t, SB_WIDTH), F32),
        scratch_shapes=[pltpu.VMEM((nq, 2 * SB_BLOCK, LANES), BF16)],
        compiler_params=_params("parallel", "parallel"),
        name="stick_breaking",
    )(sq, sk, sv, mo)


def _cumsum_matrix():
    j = jnp.arange(SB_BLOCK)[:, None]
    s = jnp.arange(SB_BLOCK)[None, :]
    return (j > s).astype(BF16)


def _mem_kv_kernel(m_ref, g_ref, w_ref, gk_ref, k_ref, v_ref):
    d = m_ref.shape[1]
    hd = d // X_HEADS
    mn = _rms(m_ref[...], g_ref[...]).astype(BF16)
    k = _dot(mn, _bf16(w_ref[:, :d]))
    for h in range(X_HEADS):
        kh = k[:, h * hd:(h + 1) * hd]
        k_ref[:, h * hd:(h + 1) * hd] = (_rms(kh, gk_ref[...]) * (hd ** -0.5)).astype(BF16)
    v_ref[...] = _dot(mn, _bf16(w_ref[:, d:])).astype(BF16)


def _mem_kv(mem, g, w, gk, l):
    t, d = mem.shape
    row = lambda i: (i, 0)
    out = pl.BlockSpec((ROW_TILE, d), row)
    return pl.pallas_call(
        _mem_kv_kernel,
        grid=(t // ROW_TILE,),
        in_specs=[pl.BlockSpec((ROW_TILE, d), row), _layer(g, l), _layer(w, l), _layer(gk, l)],
        out_specs=[out, out],
        out_shape=[jax.ShapeDtypeStruct((t, d), BF16)] * 2,
        compiler_params=_params("parallel"),
        name="mem_kv",
    )(mem, g, w, gk)


def _mix_cross_kernel(x_ref, oret_ref, osb_ref, gsb_ref, wmix_ref, gc_ref, wxq_ref, gq_ref,
                      k_ref, v_ref, wxo_ref, o_ref):
    d = x_ref.shape[1]
    hd = d // X_HEADS
    nsb = _rms(osb_ref[...], gsb_ref[...]).astype(BF16)
    x = (x_ref[...] + _dot(oret_ref[...], _bf16(wmix_ref[:RET_WIDTH, :]))
         + _dot(nsb, _bf16(wmix_ref[RET_WIDTH:, :])))
    q = _dot(_rms(x, gc_ref[...]).astype(BF16), _bf16(wxq_ref[...]))
    heads = []
    for h in range(X_HEADS):
        cols = slice(h * hd, (h + 1) * hd)
        qn = _rms(q[:, cols], gq_ref[...]).astype(BF16)
        s = _nt_dot(qn, k_ref[0, :, cols])
        e = jnp.exp(s - jnp.max(s, axis=-1, keepdims=True))
        p = (e / jnp.sum(e, axis=-1, keepdims=True)).astype(BF16)
        heads.append(_dot(p, v_ref[0, :, cols]).astype(BF16))
    o_ref[...] = x + _dot(jnp.concatenate(heads, axis=1), _bf16(wxo_ref[...]))


def _mix_cross(x, oret, osb, gsb, wmix, gc, wxq, gq, k, v, wxo, l, seq):
    t, d = x.shape
    per_batch = seq // ROW_TILE
    row = lambda i: (i, 0)
    full = lambda a: _layer(a, l)
    mem = pl.BlockSpec((1,) + k.shape[1:], lambda i: (i // per_batch, 0, 0))
    return pl.pallas_call(
        _mix_cross_kernel,
        grid=(t // ROW_TILE,),
        in_specs=[pl.BlockSpec((ROW_TILE, d), row), pl.BlockSpec((ROW_TILE, RET_WIDTH), row),
                  pl.BlockSpec((ROW_TILE, SB_WIDTH), row), full(gsb), full(wmix), full(gc),
                  full(wxq), full(gq), mem, mem, full(wxo)],
        out_specs=pl.BlockSpec((ROW_TILE, d), row),
        out_shape=jax.ShapeDtypeStruct((t, d), F32),
        compiler_params=_params("parallel"),
        name="mix_cross",
    )(x, oret, osb, gsb, wmix, gc, wxq, gq, k, v, wxo)


def _mlp_kernel(x_ref, g_ref, wup_ref, wdn_ref, o_ref, acc_ref):
    x = x_ref[...]
    hm = _rms(x, g_ref[...]).astype(BF16)
    for f in range(wup_ref.shape[1] // FF_CHUNK):
        cols = slice(f * FF_CHUNK, (f + 1) * FF_CHUNK)
        up = _dot(hm, _bf16(wup_ref[:, cols]))
        hidden = jnp.square(jnp.maximum(up, 0.0)).astype(BF16)
        down = _dot(hidden, _bf16(wdn_ref[cols, :]))
        if f:
            acc_ref[...] += down
        else:
            acc_ref[...] = down
    o_ref[...] = x + acc_ref[...]


def _mlp(x, g, wup, wdn, l):
    t, d = x.shape
    row = lambda i: (i, 0)
    return pl.pallas_call(
        _mlp_kernel,
        grid=(t // MLP_TILE,),
        in_specs=[pl.BlockSpec((MLP_TILE, d), row), _layer(g, l), _layer(wup, l), _layer(wdn, l)],
        out_specs=pl.BlockSpec((MLP_TILE, d), row),
        out_shape=jax.ShapeDtypeStruct((t, d), F32),
        scratch_shapes=[pltpu.VMEM((MLP_TILE, d), F32)],
        compiler_params=_params("parallel"),
        name="mlp",
    )(x, g, wup, wdn)


def _rotary_tables(seq):
    half = RET_HD // 2
    inv = 1.0 / (ROPE_BASE ** jnp.linspace(0.0, 1.0, half, dtype=F32))
    ang = jnp.arange(seq, dtype=F32)[:, None] * inv[None, :]
    cos, sin = jnp.cos(ang), jnp.sin(ang)
    return jnp.concatenate([cos, cos], axis=1), jnp.concatenate([-sin, sin], axis=1)


def kernel(x, mem, g_mix, w_in, g_ret_out, g_sb_out, w_mix_out, g_cross, g_mem,
           w_xq, w_xkv, g_qn, g_kn, w_xo, g_mlp, w_up, w_down):
    batch, seq, d = x.shape
    n_mem = mem.shape[1]
    depth = w_in.shape[0]
    cos, sin = _rotary_tables(seq)
    ret_tabs = _retention_tables()
    mo = _cumsum_matrix()
    xt = x.reshape(batch * seq, d)
    memt = mem.reshape(batch * n_mem, d)
    row = lambda g: g.reshape(depth, 1, -1)
    for l in range(depth):
        rq, rk, rv, rg, sq, sk, sv = _in_proj(xt, row(g_mix), w_in, l, cos, sin, seq)
        o_ret = _retention(rq, rk, rv, rg, ret_tabs, g_ret_out[l].reshape(RET_HEADS, 1, RET_HD),
                           batch, seq)
        o_sb = _stick_breaking(sq, sk, sv, mo, batch, seq)
        km, vm = _mem_kv(memt, row(g_mem), w_xkv, row(g_kn), l)
        xt = _mix_cross(xt, o_ret, o_sb, row(g_sb_out), w_mix_out, row(g_cross), w_xq, row(g_qn),
                        km.reshape(batch, n_mem, d), vm.reshape(batch, n_mem, d), w_xo, l, seq)
        xt = _mlp(xt, row(g_mlp), w_up, w_down, l)
    return xt.reshape(batch, seq, d)
```

```python
import functools
import math

import jax
import jax.numpy as jnp
from jax import lax
from jax.experimental import pallas as pl
from jax.experimental.pallas import tpu as pltpu

F32 = jnp.float32
BF16 = jnp.bfloat16

EPS = 1e-6
LOG2E = math.log2(math.e)
ROPE_BASE = 10000.0
CHUNK = 64
RET_HEADS = 4
RET_HD = 128
RET_WIDTH = RET_HEADS * RET_HD
SB_HEADS = 8
SB_HD = 64
SB_WIDTH = SB_HEADS * SB_HD
X_HEADS = 4
LANES = 128
ROW_TILE = 1024
MLP_TILE = 1024
RET_BLOCK = 256
SB_BLOCK = 128
FF_CHUNK = 512
SB_STAGE_STEP = (0, 2, 4, 5)
SB_STAGE_ORDER = (0, 1, 2, 3)
VMEM_LIMIT = 56 * 1024 * 1024


def _nt_dot(a, b):
    return lax.dot_general(a, b, (((1,), (1,)), ((), ())), preferred_element_type=F32)


def _dot(a, b):
    return jnp.dot(a, b, preferred_element_type=F32)


def _rms(x, g):
    return x * lax.rsqrt(jnp.mean(x * x, axis=-1, keepdims=True) + EPS) * g


def _params(*sem):
    return pltpu.CompilerParams(dimension_semantics=sem, vmem_limit_bytes=VMEM_LIMIT)


def _resident(a):
    return pl.BlockSpec(a.shape, lambda *_: (0,) * a.ndim, pipeline_mode=pl.Buffered(1))


def _layer(a, l):
    return pl.BlockSpec((None,) + a.shape[1:], lambda *_: (l,) + (0,) * (a.ndim - 1),
                        pipeline_mode=pl.Buffered(1))


def _bf16(w):
    return w.astype(BF16)


def _in_proj_kernel(x_ref, g_ref, w_ref, cos_ref, sin_ref, wup_ref, wdn_ref,
                    rq_ref, rk_ref, rv_ref, rg_ref, sq_ref, sk_ref, sv_ref, upb_ref, dnb_ref):
    upb_ref[...] = _bf16(wup_ref[...])
    dnb_ref[...] = _bf16(wdn_ref[...])
    h = _rms(x_ref[...], g_ref[...]).astype(BF16)
    cos = cos_ref[...]
    sin = sin_ref[...]

    def proj(i):
        return _dot(h, _bf16(w_ref[:, i * RET_WIDTH:(i + 1) * RET_WIDTH]))

    def rotary(p, scale):
        for hd in range(RET_HEADS):
            c = p[:, hd * RET_HD:(hd + 1) * RET_HD]
            r = c * cos + pltpu.roll(c, RET_HD // 2, 1) * sin
            yield hd, r * scale if scale != 1.0 else r

    for hd, r in rotary(proj(0), 1.0):
        rq_ref[:, hd * RET_HD:(hd + 1) * RET_HD] = r.astype(BF16)
    for hd, r in rotary(proj(1), RET_HD ** -0.5):
        rk_ref[:, hd * RET_HD:(hd + 1) * RET_HD] = r.astype(BF16)
    rv_ref[...] = proj(2).astype(BF16)
    rg_ref[...] = proj(3)
    sq_ref[...] = (proj(4) * (SB_HD ** -0.5)).astype(BF16)
    sk_ref[...] = proj(5).astype(BF16)
    sv_ref[...] = proj(6).astype(BF16)


def _in_proj(x, g, w, l, cos, sin, wup, wdn, seq):
    t, d = x.shape
    steps = t // ROW_TILE
    ff = wup.shape[2]
    slab = ff // steps
    n_pos = seq // ROW_TILE
    row = lambda i: (i, 0)
    half = pl.BlockSpec((ROW_TILE, RET_WIDTH), row)
    pos = pl.BlockSpec((ROW_TILE, RET_HD), lambda i: (i % n_pos, 0))
    outs = [jax.ShapeDtypeStruct((t, RET_WIDTH), dt)
            for dt in (BF16, BF16, BF16, F32, BF16, BF16, BF16)]
    outs += [jax.ShapeDtypeStruct((d, ff), BF16), jax.ShapeDtypeStruct((ff, d), BF16)]
    return pl.pallas_call(
        _in_proj_kernel,
        grid=(steps,),
        in_specs=[pl.BlockSpec((ROW_TILE, d), row), _layer(g, l), _layer(w, l), pos, pos,
                  pl.BlockSpec((None, d, slab), lambda i: (l, 0, i)),
                  pl.BlockSpec((None, slab, d), lambda i: (l, i, 0))],
        out_specs=[half] * 7 + [pl.BlockSpec((d, slab), lambda i: (0, i)),
                                pl.BlockSpec((slab, d), lambda i: (i, 0))],
        out_shape=outs,
        compiler_params=_params("parallel"),
        name="in_proj",
    )(x, g, w, cos, sin, wup, wdn)


def _retention_kernel(q_ref, k_ref, v_ref, gate_ref, w_ref, qd_ref, kd_ref, cd_ref, gn_ref, o_ref):
    state = [None] * RET_HEADS
    nb = q_ref.shape[0] // RET_BLOCK
    for blk in range(nb):
        rows = slice(blk * RET_BLOCK, (blk + 1) * RET_BLOCK)
        for h in range(RET_HEADS):
            cols = slice(h * RET_HD, (h + 1) * RET_HD)
            q = q_ref[rows, cols]
            k = k_ref[rows, cols]
            v = v_ref[rows, cols]
            p = (_nt_dot(q, k) * w_ref[h]).astype(BF16)
            o = _dot(p, v)
            if blk:
                o = o + _dot(q, state[h].astype(BF16)) * qd_ref[h]
            if blk < nb - 1:
                kd = (k.astype(F32) * kd_ref[h]).T.astype(BF16)
                update = _dot(kd, v)
                state[h] = state[h] * cd_ref[h] + update if blk else update
            gate = gate_ref[rows, cols]
            o_ref[rows, cols] = (_rms(o, gn_ref[h]) * (gate * jax.nn.sigmoid(gate))).astype(BF16)


def _retention(rq, rk, rv, rg, tabs, gn, batch, seq):
    t = rq.shape[0]
    tok = pl.BlockSpec((seq, RET_WIDTH), lambda b: (b, 0))
    return pl.pallas_call(
        _retention_kernel,
        grid=(batch,),
        in_specs=[tok, tok, tok, tok] + [_resident(a) for a in (*tabs, gn)],
        out_specs=tok,
        out_shape=jax.ShapeDtypeStruct((t, RET_WIDTH), BF16),
        compiler_params=_params("parallel"),
        name="retention",
    )(rq, rk, rv, rg, *tabs, gn)


def _retention_tables():
    lg = jnp.log1p(-jnp.exp2(-5.0 - jnp.arange(RET_HEADS, dtype=F32)))[:, None, None]
    idx = jnp.arange(RET_BLOCK, dtype=F32)
    dist = jnp.abs(idx[:, None] - idx[None, :])
    chunk = jnp.arange(RET_BLOCK) // CHUNK
    visible = chunk[None, :] <= chunk[:, None]
    w = jnp.where(visible[None], jnp.exp(lg * dist[None]), 0.0)
    ones = jnp.ones((1, 1, RET_HD), F32)
    qd = jnp.exp(lg * (idx + 1.0)[None, :, None]) * ones
    kd = jnp.exp(lg * (RET_BLOCK - 1.0 - idx)[None, :, None]) * ones
    cd = jnp.exp(lg * RET_BLOCK) * ones
    return w, qd, kd, cd


def _stick_breaking_kernel(q_ref, k_ref, v_ref, mo_ref, o_ref, q2_ref):
    nq = q_ref.shape[0] // SB_BLOCK
    lane = lax.broadcasted_iota(jnp.int32, (SB_BLOCK, LANES), 1)
    first = lane < SB_HD
    rows = lambda blk: slice(blk * SB_BLOCK, (blk + 1) * SB_BLOCK)
    for blk in range(nq):
        q = q_ref[rows(blk), :].astype(F32)
        q2_ref[blk] = jnp.concatenate([jnp.where(first, q, 0.0), jnp.where(first, 0.0, q)],
                                      axis=0).astype(BF16)
    row = lax.broadcasted_iota(jnp.int32, (2 * SB_BLOCK, SB_BLOCK), 0) % SB_BLOCK
    col = lax.broadcasted_iota(jnp.int32, (2 * SB_BLOCK, SB_BLOCK), 1)
    strict = col < row

    tiles = [(blk, kb, kb == blk) for blk in range(nq) for kb in range(blk, -1, -1)]
    carry = {}
    acc = {}

    def scores(blk, kb, diag):
        return _nt_dot(q2_ref[blk], k_ref[rows(kb), :])

    def logs(blk, kb, diag, z):
        log_beta = jnp.minimum(z, 0.0) - jnp.log(1.0 + jnp.exp2(jnp.abs(z) * -LOG2E))
        log_1m = log_beta - z
        if diag:
            log_1m = jnp.where(strict, log_1m, 0.0)
        if not diag:
            log_beta = log_beta + carry[blk]
        if kb:
            total = jnp.sum(log_1m, axis=-1, keepdims=True)
            carry[blk] = total if diag else carry[blk] + total
        return log_beta, _dot(log_1m.astype(BF16), mo_ref[...])

    def weights(blk, kb, diag, log_beta, after):
        a = jnp.exp2((log_beta + after) * LOG2E)
        if diag:
            a = jnp.where(strict, a, 0.0)
        return _dot(a.astype(BF16), v_ref[rows(kb), :])

    def accumulate(blk, kb, diag, pv):
        acc[blk] = pv if diag else acc[blk] + pv
        if kb == 0:
            out = acc.pop(blk)
            o_ref[rows(blk), :] = jnp.where(first, out[:SB_BLOCK, :], out[SB_BLOCK:, :])

    stages = (scores, logs, weights, accumulate)
    live = {}
    for step in range(len(tiles) + SB_STAGE_STEP[-1]):
        for depth in SB_STAGE_ORDER:
            t = step - SB_STAGE_STEP[depth]
            if 0 <= t < len(tiles):
                prev = live.pop((t, depth - 1), ())
                out = stages[depth](*tiles[t], *prev)
                live[(t, depth)] = out if isinstance(out, tuple) else (out,)


def _stick_breaking(sq, sk, sv, mo, batch, seq):
    t = sq.shape[0]
    nq = seq // SB_BLOCK
    pairs = SB_WIDTH // LANES
    spec = pl.BlockSpec((seq, LANES), lambda b, c: (b, c))
    return pl.pallas_call(
        _stick_breaking_kernel,
        grid=(batch, pairs),
        in_specs=[spec, spec, spec, pl.BlockSpec(mo.shape, lambda b, c: (0, 0))],
        out_specs=spec,
        out_shape=jax.ShapeDtypeStruct((t, SB_WIDTH), F32),
        scratch_shapes=[pltpu.VMEM((nq, 2 * SB_BLOCK, LANES), BF16)],
        compiler_params=_params("parallel", "parallel"),
        name="stick_breaking",
    )(sq, sk, sv, mo)


def _cumsum_matrix():
    j = jnp.arange(SB_BLOCK)[:, None]
    s = jnp.arange(SB_BLOCK)[None, :]
    return (j > s).astype(BF16)


def _mem_kv_kernel(m_ref, g_ref, w_ref, gk_ref, k_ref, v_ref):
    d = m_ref.shape[1]
    hd = d // X_HEADS
    mn = _rms(m_ref[...], g_ref[...]).astype(BF16)
    k = _dot(mn, _bf16(w_ref[:, :d]))
    for h in range(X_HEADS):
        kh = k[:, h * hd:(h + 1) * hd]
        k_ref[:, h * hd:(h + 1) * hd] = (_rms(kh, gk_ref[...]) * (hd ** -0.5)).astype(BF16)
    v_ref[...] = _dot(mn, _bf16(w_ref[:, d:])).astype(BF16)


def _mem_kv(mem, g, w, gk, l):
    t, d = mem.shape
    row = lambda i: (i, 0)
    out = pl.BlockSpec((ROW_TILE, d), row)
    return pl.pallas_call(
        _mem_kv_kernel,
        grid=(t // ROW_TILE,),
        in_specs=[pl.BlockSpec((ROW_TILE, d), row), _layer(g, l), _layer(w, l), _layer(gk, l)],
        out_specs=[out, out],
        out_shape=[jax.ShapeDtypeStruct((t, d), BF16)] * 2,
        compiler_params=_params("parallel"),
        name="mem_kv",
    )(mem, g, w, gk)


def _mix_cross_kernel(x_ref, oret_ref, osb_ref, gsb_ref, wmix_ref, gc_ref, wxq_ref, gq_ref,
                      k_ref, v_ref, wxo_ref, o_ref):
    d = x_ref.shape[1]
    hd = d // X_HEADS
    nsb = _rms(osb_ref[...], gsb_ref[...]).astype(BF16)
    x = (x_ref[...] + _dot(oret_ref[...], _bf16(wmix_ref[:RET_WIDTH, :]))
         + _dot(nsb, _bf16(wmix_ref[RET_WIDTH:, :])))
    q = _dot(_rms(x, gc_ref[...]).astype(BF16), _bf16(wxq_ref[...]))
    heads = []
    for h in range(X_HEADS):
        cols = slice(h * hd, (h + 1) * hd)
        qn = _rms(q[:, cols], gq_ref[...]).astype(BF16)
        s = _nt_dot(qn, k_ref[0, :, cols])
        e = jnp.exp(s - jnp.max(s, axis=-1, keepdims=True))
        p = (e / jnp.sum(e, axis=-1, keepdims=True)).astype(BF16)
        heads.append(_dot(p, v_ref[0, :, cols]).astype(BF16))
    o_ref[...] = x + _dot(jnp.concatenate(heads, axis=1), _bf16(wxo_ref[...]))


def _mix_cross(x, oret, osb, gsb, wmix, gc, wxq, gq, k, v, wxo, l, seq):
    t, d = x.shape
    per_batch = seq // ROW_TILE
    row = lambda i: (i, 0)
    full = lambda a: _layer(a, l)
    mem = pl.BlockSpec((1,) + k.shape[1:], lambda i: (i // per_batch, 0, 0))
    return pl.pallas_call(
        _mix_cross_kernel,
        grid=(t // ROW_TILE,),
        in_specs=[pl.BlockSpec((ROW_TILE, d), row), pl.BlockSpec((ROW_TILE, RET_WIDTH), row),
                  pl.BlockSpec((ROW_TILE, SB_WIDTH), row), full(gsb), full(wmix), full(gc),
                  full(wxq), full(gq), mem, mem, full(wxo)],
        out_specs=pl.BlockSpec((ROW_TILE, d), row),
        out_shape=jax.ShapeDtypeStruct((t, d), F32),
        compiler_params=_params("parallel"),
        name="mix_cross",
    )(x, oret, osb, gsb, wmix, gc, wxq, gq, k, v, wxo)


def _mlp_kernel(x_ref, g_ref, wup_ref, wdn_ref, o_ref, acc_ref):
    x = x_ref[...]
    hm = _rms(x, g_ref[...]).astype(BF16)
    for f in range(wup_ref.shape[1] // FF_CHUNK):
        cols = slice(f * FF_CHUNK, (f + 1) * FF_CHUNK)
        up = _dot(hm, wup_ref[:, cols])
        hidden = jnp.square(jnp.maximum(up, 0.0)).astype(BF16)
        down = _dot(hidden, wdn_ref[cols, :])
        if f:
            acc_ref[...] += down
        else:
            acc_ref[...] = down
    o_ref[...] = x + acc_ref[...]


def _mlp(x, g, wup, wdn, l):
    t, d = x.shape
    row = lambda i: (i, 0)
    return pl.pallas_call(
        _mlp_kernel,
        grid=(t // MLP_TILE,),
        in_specs=[pl.BlockSpec((MLP_TILE, d), row), _layer(g, l), _resident(wup), _resident(wdn)],
        out_specs=pl.BlockSpec((MLP_TILE, d), row),
        out_shape=jax.ShapeDtypeStruct((t, d), F32),
        scratch_shapes=[pltpu.VMEM((MLP_TILE, d), F32)],
        compiler_params=_params("parallel"),
        name="mlp",
    )(x, g, wup, wdn)


def _rotary_tables(seq):
    half = RET_HD // 2
    inv = 1.0 / (ROPE_BASE ** jnp.linspace(0.0, 1.0, half, dtype=F32))
    ang = jnp.arange(seq, dtype=F32)[:, None] * inv[None, :]
    cos, sin = jnp.cos(ang), jnp.sin(ang)
    return jnp.concatenate([cos, cos], axis=1), jnp.concatenate([-sin, sin], axis=1)


def kernel(x, mem, g_mix, w_in, g_ret_out, g_sb_out, w_mix_out, g_cross, g_mem,
           w_xq, w_xkv, g_qn, g_kn, w_xo, g_mlp, w_up, w_down):
    batch, seq, d = x.shape
    n_mem = mem.shape[1]
    depth = w_in.shape[0]
    cos, sin = _rotary_tables(seq)
    ret_tabs = _retention_tables()
    mo = _cumsum_matrix()
    xt = x.reshape(batch * seq, d)
    memt = mem.reshape(batch * n_mem, d)
    row = lambda g: g.reshape(depth, 1, -1)
    for l in range(depth):
        rq, rk, rv, rg, sq, sk, sv, up_bf16, down_bf16 = _in_proj(
            xt, row(g_mix), w_in, l, cos, sin, w_up, w_down, seq)
        o_ret = _retention(rq, rk, rv, rg, ret_tabs, g_ret_out[l].reshape(RET_HEADS, 1, RET_HD),
                           batch, seq)
        o_sb = _stick_breaking(sq, sk, sv, mo, batch, seq)
        km, vm = _mem_kv(memt, row(g_mem), w_xkv, row(g_kn), l)
        xt = _mix_cross(xt, o_ret, o_sb, row(g_sb_out), w_mix_out, row(g_cross), w_xq, row(g_qn),
                        km.reshape(batch, n_mem, d), vm.reshape(batch, n_mem, d), w_xo, l, seq)
        xt = _mlp(xt, row(g_mlp), up_bf16, down_bf16, l)
    return xt.reshape(batch, seq, d)
```

```python
import functools
import math

import jax
import jax.numpy as jnp
from jax import lax
from jax.experimental import pallas as pl
from jax.experimental.pallas import tpu as pltpu

F32 = jnp.float32
BF16 = jnp.bfloat16

EPS = 1e-6
LOG2E = math.log2(math.e)
ROPE_BASE = 10000.0
CHUNK = 64
RET_HEADS = 4
RET_HD = 128
RET_WIDTH = RET_HEADS * RET_HD
SB_HEADS = 8
SB_HD = 64
SB_WIDTH = SB_HEADS * SB_HD
X_HEADS = 4
LANES = 128
ROW_TILE = 1024
MLP_TILE = 1024
RET_BLOCK = 256
SB_BLOCK = 128
FF_CHUNK = 512
SB_STAGE_STEP = (0, 2, 4, 5)
SB_STAGE_ORDER = (0, 1, 2, 3)
VMEM_LIMIT = 56 * 1024 * 1024


def _nt_dot(a, b):
    return lax.dot_general(a, b, (((1,), (1,)), ((), ())), preferred_element_type=F32)


def _dot(a, b):
    return jnp.dot(a, b, preferred_element_type=F32)


def _rms(x, g):
    return x * lax.rsqrt(jnp.mean(x * x, axis=-1, keepdims=True) + EPS) * g


def _params(*sem):
    return pltpu.CompilerParams(dimension_semantics=sem, vmem_limit_bytes=VMEM_LIMIT)


def _resident(a):
    return pl.BlockSpec(a.shape, lambda *_: (0,) * a.ndim, pipeline_mode=pl.Buffered(1))


def _layer(a, l):
    return pl.BlockSpec((None,) + a.shape[1:], lambda *_: (l,) + (0,) * (a.ndim - 1),
                        pipeline_mode=pl.Buffered(1))


def _bf16(w):
    return w.astype(BF16)


def _in_proj_kernel(x_ref, g_ref, w_ref, cos_ref, sin_ref, wup_ref, wdn_ref,
                    dw_ref, qd_ref, kd_ref, cd_ref, gn_ref,
                    oret_ref, sq_ref, sk_ref, sv_ref, upb_ref, dnb_ref, state_ref):
    upb_ref[...] = _bf16(wup_ref[...])
    dnb_ref[...] = _bf16(wdn_ref[...])
    tile = pl.program_id(0) % (cos_ref.shape[0] // ROW_TILE)

    @pl.when(tile == 0)
    def _():
        state_ref[...] = jnp.zeros_like(state_ref)

    h = _rms(x_ref[...], g_ref[...]).astype(BF16)
    heads = [slice(hd * RET_HD, (hd + 1) * RET_HD) for hd in range(RET_HEADS)]

    def project(blk):
        rows = slice(blk * RET_BLOCK, (blk + 1) * RET_BLOCK)
        pos = pl.ds(pl.multiple_of(tile * ROW_TILE + blk * RET_BLOCK, RET_BLOCK), RET_BLOCK)
        cos = cos_ref[pos, :]
        sin = sin_ref[pos, :]
        hb = h[rows]
        proj = lambda i: _dot(hb, _bf16(w_ref[:, i * RET_WIDTH:(i + 1) * RET_WIDTH]))
        rot = lambda c: c * cos + pltpu.roll(c, RET_HD // 2, 1) * sin
        p = proj(0)
        q = [rot(p[:, c]).astype(BF16) for c in heads]
        p = proj(1)
        k = [(rot(p[:, c]) * (RET_HD ** -0.5)).astype(BF16) for c in heads]
        v = proj(2).astype(BF16)
        gate = proj(3)
        sq_ref[rows, :] = (proj(4) * (SB_HD ** -0.5)).astype(BF16)
        sk_ref[rows, :] = proj(5).astype(BF16)
        sv_ref[rows, :] = proj(6).astype(BF16)
        return q, k, v, gate

    def retain(blk, q, k, v, gate):
        rows = slice(blk * RET_BLOCK, (blk + 1) * RET_BLOCK)
        for hd, cols in enumerate(heads):
            state = state_ref[hd]
            p = (_nt_dot(q[hd], k[hd]) * dw_ref[hd]).astype(BF16)
            o = _dot(p, v[:, cols]) + _dot(q[hd], state.astype(BF16)) * qd_ref[hd]
            kd = (k[hd].astype(F32) * kd_ref[hd]).T.astype(BF16)
            state_ref[hd] = state * cd_ref[hd] + _dot(kd, v[:, cols])
            gt = gate[:, cols]
            oret_ref[rows, cols] = (_rms(o, gn_ref[hd]) * (gt * jax.nn.sigmoid(gt))).astype(BF16)

    blocks = ROW_TILE // RET_BLOCK
    ready = project(0)
    for blk in range(blocks):
        ahead = project(blk + 1) if blk + 1 < blocks else None
        retain(blk, *ready)
        ready = ahead


def _in_proj(x, g, w, l, cos, sin, wup, wdn, tabs, gn, seq):
    t, d = x.shape
    steps = t // ROW_TILE
    ff = wup.shape[2]
    slab = ff // steps
    row = lambda i: (i, 0)
    half = pl.BlockSpec((ROW_TILE, RET_WIDTH), row)
    outs = [jax.ShapeDtypeStruct((t, RET_WIDTH), BF16)] * 4
    outs += [jax.ShapeDtypeStruct((d, ff), BF16), jax.ShapeDtypeStruct((ff, d), BF16)]
    return pl.pallas_call(
        _in_proj_kernel,
        grid=(steps,),
        in_specs=[pl.BlockSpec((ROW_TILE, d), row), _layer(g, l), _layer(w, l),
                  _resident(cos), _resident(sin),
                  pl.BlockSpec((None, d, slab), lambda i: (l, 0, i)),
                  pl.BlockSpec((None, slab, d), lambda i: (l, i, 0))]
        + [_resident(a) for a in tabs] + [_layer(gn, l)],
        out_specs=[half] * 4 + [pl.BlockSpec((d, slab), lambda i: (0, i)),
                                pl.BlockSpec((slab, d), lambda i: (i, 0))],
        out_shape=outs,
        scratch_shapes=[pltpu.VMEM((RET_HEADS, RET_HD, RET_HD), F32)],
        compiler_params=_params("arbitrary"),
        name="in_proj",
    )(x, g, w, cos, sin, wup, wdn, *tabs, gn)


def _retention_tables():
    lg = jnp.log1p(-jnp.exp2(-5.0 - jnp.arange(RET_HEADS, dtype=F32)))[:, None, None]
    idx = jnp.arange(RET_BLOCK, dtype=F32)
    dist = jnp.abs(idx[:, None] - idx[None, :])
    chunk = jnp.arange(RET_BLOCK) // CHUNK
    visible = chunk[None, :] <= chunk[:, None]
    w = jnp.where(visible[None], jnp.exp(lg * dist[None]), 0.0)
    ones = jnp.ones((1, 1, RET_HD), F32)
    qd = jnp.exp(lg * (idx + 1.0)[None, :, None]) * ones
    kd = jnp.exp(lg * (RET_BLOCK - 1.0 - idx)[None, :, None]) * ones
    cd = jnp.exp(lg * RET_BLOCK) * ones
    return w, qd, kd, cd


def _stick_breaking_kernel(q_ref, k_ref, v_ref, mo_ref, o_ref, q2_ref):
    nq = q_ref.shape[0] // SB_BLOCK
    lane = lax.broadcasted_iota(jnp.int32, (SB_BLOCK, LANES), 1)
    first = lane < SB_HD
    rows = lambda blk: slice(blk * SB_BLOCK, (blk + 1) * SB_BLOCK)
    for blk in range(nq):
        q = q_ref[rows(blk), :].astype(F32)
        q2_ref[blk] = jnp.concatenate([jnp.where(first, q, 0.0), jnp.where(first, 0.0, q)],
                                      axis=0).astype(BF16)
    row = lax.broadcasted_iota(jnp.int32, (2 * SB_BLOCK, SB_BLOCK), 0) % SB_BLOCK
    col = lax.broadcasted_iota(jnp.int32, (2 * SB_BLOCK, SB_BLOCK), 1)
    strict = col < row

    tiles = [(blk, kb, kb == blk) for blk in range(nq) for kb in range(blk, -1, -1)]
    carry = {}
    acc = {}

    def scores(blk, kb, diag):
        return _nt_dot(q2_ref[blk], k_ref[rows(kb), :])

    def logs(blk, kb, diag, z):
        log_beta = jnp.minimum(z, 0.0) - jnp.log(1.0 + jnp.exp2(jnp.abs(z) * -LOG2E))
        log_1m = log_beta - z
        if diag:
            log_1m = jnp.where(strict, log_1m, 0.0)
        if not diag:
            log_beta = log_beta + carry[blk]
        if kb:
            total = jnp.sum(log_1m, axis=-1, keepdims=True)
            carry[blk] = total if diag else carry[blk] + total
        return log_beta, _dot(log_1m.astype(BF16), mo_ref[...])

    def weights(blk, kb, diag, log_beta, after):
        a = jnp.exp2((log_beta + after) * LOG2E)
        if diag:
            a = jnp.where(strict, a, 0.0)
        return _dot(a.astype(BF16), v_ref[rows(kb), :])

    def accumulate(blk, kb, diag, pv):
        acc[blk] = pv if diag else acc[blk] + pv
        if kb == 0:
            out = acc.pop(blk)
            o_ref[rows(blk), :] = jnp.where(first, out[:SB_BLOCK, :], out[SB_BLOCK:, :])

    stages = (scores, logs, weights, accumulate)
    live = {}
    for step in range(len(tiles) + SB_STAGE_STEP[-1]):
        for depth in SB_STAGE_ORDER:
            t = step - SB_STAGE_STEP[depth]
            if 0 <= t < len(tiles):
                prev = live.pop((t, depth - 1), ())
                out = stages[depth](*tiles[t], *prev)
                live[(t, depth)] = out if isinstance(out, tuple) else (out,)


def _stick_breaking(sq, sk, sv, mo, batch, seq):
    t = sq.shape[0]
    nq = seq // SB_BLOCK
    pairs = SB_WIDTH // LANES
    spec = pl.BlockSpec((seq, LANES), lambda b, c: (b, c))
    return pl.pallas_call(
        _stick_breaking_kernel,
        grid=(batch, pairs),
        in_specs=[spec, spec, spec, pl.BlockSpec(mo.shape, lambda b, c: (0, 0))],
        out_specs=spec,
        out_shape=jax.ShapeDtypeStruct((t, SB_WIDTH), F32),
        scratch_shapes=[pltpu.VMEM((nq, 2 * SB_BLOCK, LANES), BF16)],
        compiler_params=_params("parallel", "parallel"),
        name="stick_breaking",
    )(sq, sk, sv, mo)


def _cumsum_matrix():
    j = jnp.arange(SB_BLOCK)[:, None]
    s = jnp.arange(SB_BLOCK)[None, :]
    return (j > s).astype(BF16)


def _mem_kv_kernel(m_ref, g_ref, w_ref, gk_ref, k_ref, v_ref):
    d = m_ref.shape[1]
    hd = d // X_HEADS
    mn = _rms(m_ref[...], g_ref[...]).astype(BF16)
    k = _dot(mn, _bf16(w_ref[:, :d]))
    for h in range(X_HEADS):
        kh = k[:, h * hd:(h + 1) * hd]
        k_ref[:, h * hd:(h + 1) * hd] = (_rms(kh, gk_ref[...]) * (hd ** -0.5)).astype(BF16)
    v_ref[...] = _dot(mn, _bf16(w_ref[:, d:])).astype(BF16)


def _mem_kv(mem, g, w, gk, l):
    t, d = mem.shape
    row = lambda i: (i, 0)
    out = pl.BlockSpec((ROW_TILE, d), row)
    return pl.pallas_call(
        _mem_kv_kernel,
        grid=(t // ROW_TILE,),
        in_specs=[pl.BlockSpec((ROW_TILE, d), row), _layer(g, l), _layer(w, l), _layer(gk, l)],
        out_specs=[out, out],
        out_shape=[jax.ShapeDtypeStruct((t, d), BF16)] * 2,
        compiler_params=_params("parallel"),
        name="mem_kv",
    )(mem, g, w, gk)


def _mix_cross_kernel(x_ref, oret_ref, osb_ref, gsb_ref, wmix_ref, gc_ref, wxq_ref, gq_ref,
                      k_ref, v_ref, wxo_ref, o_ref):
    d = x_ref.shape[1]
    hd = d // X_HEADS
    nsb = _rms(osb_ref[...], gsb_ref[...]).astype(BF16)
    x = (x_ref[...] + _dot(oret_ref[...], _bf16(wmix_ref[:RET_WIDTH, :]))
         + _dot(nsb, _bf16(wmix_ref[RET_WIDTH:, :])))
    q = _dot(_rms(x, gc_ref[...]).astype(BF16), _bf16(wxq_ref[...]))
    heads = []
    for h in range(X_HEADS):
        cols = slice(h * hd, (h + 1) * hd)
        qn = _rms(q[:, cols], gq_ref[...]).astype(BF16)
        s = _nt_dot(qn, k_ref[0, :, cols])
        e = jnp.exp(s - jnp.max(s, axis=-1, keepdims=True))
        p = (e / jnp.sum(e, axis=-1, keepdims=True)).astype(BF16)
        heads.append(_dot(p, v_ref[0, :, cols]).astype(BF16))
    o_ref[...] = x + _dot(jnp.concatenate(heads, axis=1), _bf16(wxo_ref[...]))


def _mix_cross(x, oret, osb, gsb, wmix, gc, wxq, gq, k, v, wxo, l, seq):
    t, d = x.shape
    per_batch = seq // ROW_TILE
    row = lambda i: (i, 0)
    full = lambda a: _layer(a, l)
    mem = pl.BlockSpec((1,) + k.shape[1:], lambda i: (i // per_batch, 0, 0))
    return pl.pallas_call(
        _mix_cross_kernel,
        grid=(t // ROW_TILE,),
        in_specs=[pl.BlockSpec((ROW_TILE, d), row), pl.BlockSpec((ROW_TILE, RET_WIDTH), row),
                  pl.BlockSpec((ROW_TILE, SB_WIDTH), row), full(gsb), full(wmix), full(gc),
                  full(wxq), full(gq), mem, mem, full(wxo)],
        out_specs=pl.BlockSpec((ROW_TILE, d), row),
        out_shape=jax.ShapeDtypeStruct((t, d), F32),
        compiler_params=_params("parallel"),
        name="mix_cross",
    )(x, oret, osb, gsb, wmix, gc, wxq, gq, k, v, wxo)


def _mlp_kernel(x_ref, g_ref, wup_ref, wdn_ref, o_ref, acc_ref):
    x = x_ref[...]
    hm = _rms(x, g_ref[...]).astype(BF16)
    for f in range(wup_ref.shape[1] // FF_CHUNK):
        cols = slice(f * FF_CHUNK, (f + 1) * FF_CHUNK)
        up = _dot(hm, wup_ref[:, cols])
        hidden = jnp.square(jnp.maximum(up, 0.0)).astype(BF16)
        down = _dot(hidden, wdn_ref[cols, :])
        if f:
            acc_ref[...] += down
        else:
            acc_ref[...] = down
    o_ref[...] = x + acc_ref[...]


def _mlp(x, g, wup, wdn, l):
    t, d = x.shape
    row = lambda i: (i, 0)
    return pl.pallas_call(
        _mlp_kernel,
        grid=(t // MLP_TILE,),
        in_specs=[pl.BlockSpec((MLP_TILE, d), row), _layer(g, l), _resident(wup), _resident(wdn)],
        out_specs=pl.BlockSpec((MLP_TILE, d), row),
        out_shape=jax.ShapeDtypeStruct((t, d), F32),
        scratch_shapes=[pltpu.VMEM((MLP_TILE, d), F32)],
        compiler_params=_params("parallel"),
        name="mlp",
    )(x, g, wup, wdn)


def _rotary_tables(seq):
    half = RET_HD // 2
    inv = 1.0 / (ROPE_BASE ** jnp.linspace(0.0, 1.0, half, dtype=F32))
    ang = jnp.arange(seq, dtype=F32)[:, None] * inv[None, :]
    cos, sin = jnp.cos(ang), jnp.sin(ang)
    return jnp.concatenate([cos, cos], axis=1), jnp.concatenate([-sin, sin], axis=1)


def kernel(x, mem, g_mix, w_in, g_ret_out, g_sb_out, w_mix_out, g_cross, g_mem,
           w_xq, w_xkv, g_qn, g_kn, w_xo, g_mlp, w_up, w_down):
    batch, seq, d = x.shape
    n_mem = mem.shape[1]
    depth = w_in.shape[0]
    cos, sin = _rotary_tables(seq)
    ret_tabs = _retention_tables()
    mo = _cumsum_matrix()
    xt = x.reshape(batch * seq, d)
    memt = mem.reshape(batch * n_mem, d)
    row = lambda g: g.reshape(depth, 1, -1)
    for l in range(depth):
        o_ret, sq, sk, sv, up_bf16, down_bf16 = _in_proj(
            xt, row(g_mix), w_in, l, cos, sin, w_up, w_down, ret_tabs,
            g_ret_out.reshape(depth, RET_HEADS, 1, RET_HD), seq)
        o_sb = _stick_breaking(sq, sk, sv, mo, batch, seq)
        km, vm = _mem_kv(memt, row(g_mem), w_xkv, row(g_kn), l)
        xt = _mix_cross(xt, o_ret, o_sb, row(g_sb_out), w_mix_out, row(g_cross), w_xq, row(g_qn),
                        km.reshape(batch, n_mem, d), vm.reshape(batch, n_mem, d), w_xo, l, seq)
        xt = _mlp(xt, row(g_mlp), up_bf16, down_bf16, l)
    return xt.reshape(batch, seq, d)
```

```python
import functools
import math

import jax
import jax.numpy as jnp
from jax import lax
from jax.experimental import pallas as pl
from jax.experimental.pallas import tpu as pltpu

F32 = jnp.float32
BF16 = jnp.bfloat16

EPS = 1e-6
LOG2E = math.log2(math.e)
ROPE_BASE = 10000.0
CHUNK = 64
RET_HEADS = 4
RET_HD = 128
RET_WIDTH = RET_HEADS * RET_HD
SB_HEADS = 8
SB_HD = 64
SB_WIDTH = SB_HEADS * SB_HD
X_HEADS = 4
LANES = 128
ROW_TILE = 1024
MLP_TILE = 1024
RET_BLOCK = 256
PROJ_BLOCK = 512
SB_BLOCK = 128
FF_CHUNK = 512
SB_STAGE_STEP = (0, 2, 4, 5)
SB_STAGE_ORDER = (0, 1, 2, 3)
VMEM_LIMIT = 56 * 1024 * 1024


def _nt_dot(a, b):
    return lax.dot_general(a, b, (((1,), (1,)), ((), ())), preferred_element_type=F32)


def _dot(a, b):
    return jnp.dot(a, b, preferred_element_type=F32)


def _rms(x, g):
    return x * lax.rsqrt(jnp.mean(x * x, axis=-1, keepdims=True) + EPS) * g


def _params(*sem):
    return pltpu.CompilerParams(dimension_semantics=sem, vmem_limit_bytes=VMEM_LIMIT)


def _resident(a):
    return pl.BlockSpec(a.shape, lambda *_: (0,) * a.ndim, pipeline_mode=pl.Buffered(1))


def _layer(a, l):
    return pl.BlockSpec((None,) + a.shape[1:], lambda *_: (l,) + (0,) * (a.ndim - 1),
                        pipeline_mode=pl.Buffered(1))


def _bf16(w):
    return w.astype(BF16)


def _in_proj_kernel(x_ref, g_ref, w_ref, cos_ref, sin_ref, wup_ref, wdn_ref,
                    dw_ref, qd_ref, kd_ref, cd_ref, gn_ref,
                    oret_ref, sq_ref, sk_ref, sv_ref, upb_ref, dnb_ref, state_ref):
    upb_ref[...] = _bf16(wup_ref[...])
    dnb_ref[...] = _bf16(wdn_ref[...])
    tile = pl.program_id(0) % (cos_ref.shape[0] // ROW_TILE)

    @pl.when(tile == 0)
    def _():
        state_ref[...] = jnp.zeros_like(state_ref)

    h = _rms(x_ref[...], g_ref[...]).astype(BF16)
    heads = [slice(hd * RET_HD, (hd + 1) * RET_HD) for hd in range(RET_HEADS)]

    def project(blk):
        rows = slice(blk * PROJ_BLOCK, (blk + 1) * PROJ_BLOCK)
        pos = pl.ds(pl.multiple_of(tile * ROW_TILE + blk * PROJ_BLOCK, PROJ_BLOCK), PROJ_BLOCK)
        cos = cos_ref[pos, :]
        sin = sin_ref[pos, :]
        hb = h[rows]
        proj = lambda i: _dot(hb, _bf16(w_ref[:, i * RET_WIDTH:(i + 1) * RET_WIDTH]))
        rot = lambda c: c * cos + pltpu.roll(c, RET_HD // 2, 1) * sin
        p = proj(0)
        q = [rot(p[:, c]).astype(BF16) for c in heads]
        p = proj(1)
        k = [(rot(p[:, c]) * (RET_HD ** -0.5)).astype(BF16) for c in heads]
        v = proj(2).astype(BF16)
        gate = proj(3)
        sq_ref[rows, :] = (proj(4) * (SB_HD ** -0.5)).astype(BF16)
        sk_ref[rows, :] = proj(5).astype(BF16)
        sv_ref[rows, :] = proj(6).astype(BF16)
        return q, k, v, gate

    def retain(blk, q, k, v, gate):
        for sub in range(PROJ_BLOCK // RET_BLOCK):
            part = slice(sub * RET_BLOCK, (sub + 1) * RET_BLOCK)
            rows = slice(blk * PROJ_BLOCK + part.start, blk * PROJ_BLOCK + part.stop)
            for hd, cols in enumerate(heads):
                qb, kb, vb = q[hd][part], k[hd][part], v[part, cols]
                state = state_ref[hd]
                p = (_nt_dot(qb, kb) * dw_ref[hd]).astype(BF16)
                o = _dot(p, vb) + _dot(qb, state.astype(BF16)) * qd_ref[hd]
                kd = (kb.astype(F32) * kd_ref[hd]).T.astype(BF16)
                state_ref[hd] = state * cd_ref[hd] + _dot(kd, vb)
                gt = gate[part, cols]
                oret_ref[rows, cols] = (_rms(o, gn_ref[hd])
                                        * (gt * jax.nn.sigmoid(gt))).astype(BF16)

    blocks = ROW_TILE // PROJ_BLOCK
    ready = project(0)
    for blk in range(blocks):
        ahead = project(blk + 1) if blk + 1 < blocks else None
        retain(blk, *ready)
        ready = ahead


def _in_proj(x, g, w, l, cos, sin, wup, wdn, tabs, gn, seq):
    t, d = x.shape
    steps = t // ROW_TILE
    ff = wup.shape[2]
    slab = ff // steps
    row = lambda i: (i, 0)
    half = pl.BlockSpec((ROW_TILE, RET_WIDTH), row)
    outs = [jax.ShapeDtypeStruct((t, RET_WIDTH), BF16)] * 4
    outs += [jax.ShapeDtypeStruct((d, ff), BF16), jax.ShapeDtypeStruct((ff, d), BF16)]
    return pl.pallas_call(
        _in_proj_kernel,
        grid=(steps,),
        in_specs=[pl.BlockSpec((ROW_TILE, d), row), _layer(g, l), _layer(w, l),
                  _resident(cos), _resident(sin),
                  pl.BlockSpec((None, d, slab), lambda i: (l, 0, i)),
                  pl.BlockSpec((None, slab, d), lambda i: (l, i, 0))]
        + [_resident(a) for a in tabs] + [_layer(gn, l)],
        out_specs=[half] * 4 + [pl.BlockSpec((d, slab), lambda i: (0, i)),
                                pl.BlockSpec((slab, d), lambda i: (i, 0))],
        out_shape=outs,
        scratch_shapes=[pltpu.VMEM((RET_HEADS, RET_HD, RET_HD), F32)],
        compiler_params=_params("arbitrary"),
        name="in_proj",
    )(x, g, w, cos, sin, wup, wdn, *tabs, gn)


def _retention_tables():
    lg = jnp.log1p(-jnp.exp2(-5.0 - jnp.arange(RET_HEADS, dtype=F32)))[:, None, None]
    idx = jnp.arange(RET_BLOCK, dtype=F32)
    dist = jnp.abs(idx[:, None] - idx[None, :])
    chunk = jnp.arange(RET_BLOCK) // CHUNK
    visible = chunk[None, :] <= chunk[:, None]
    w = jnp.where(visible[None], jnp.exp(lg * dist[None]), 0.0)
    ones = jnp.ones((1, 1, RET_HD), F32)
    qd = jnp.exp(lg * (idx + 1.0)[None, :, None]) * ones
    kd = jnp.exp(lg * (RET_BLOCK - 1.0 - idx)[None, :, None]) * ones
    cd = jnp.exp(lg * RET_BLOCK) * ones
    return w, qd, kd, cd


def _stick_breaking_kernel(q_ref, k_ref, v_ref, mo_ref, o_ref, q2_ref):
    nq = q_ref.shape[0] // SB_BLOCK
    lane = lax.broadcasted_iota(jnp.int32, (SB_BLOCK, LANES), 1)
    first = lane < SB_HD
    rows = lambda blk: slice(blk * SB_BLOCK, (blk + 1) * SB_BLOCK)
    for blk in range(nq):
        q = q_ref[rows(blk), :].astype(F32)
        q2_ref[blk] = jnp.concatenate([jnp.where(first, q, 0.0), jnp.where(first, 0.0, q)],
                                      axis=0).astype(BF16)
    row = lax.broadcasted_iota(jnp.int32, (2 * SB_BLOCK, SB_BLOCK), 0) % SB_BLOCK
    col = lax.broadcasted_iota(jnp.int32, (2 * SB_BLOCK, SB_BLOCK), 1)
    strict = col < row

    tiles = [(blk, kb, kb == blk) for blk in range(nq) for kb in range(blk, -1, -1)]
    carry = {}
    acc = {}

    def scores(blk, kb, diag):
        return _nt_dot(q2_ref[blk], k_ref[rows(kb), :])

    def logs(blk, kb, diag, z):
        log_beta = jnp.minimum(z, 0.0) - jnp.log(1.0 + jnp.exp2(jnp.abs(z) * -LOG2E))
        log_1m = log_beta - z
        if diag:
            log_1m = jnp.where(strict, log_1m, 0.0)
        if not diag:
            log_beta = log_beta + carry[blk]
        if kb:
            total = jnp.sum(log_1m, axis=-1, keepdims=True)
            carry[blk] = total if diag else carry[blk] + total
        return log_beta, _dot(log_1m.astype(BF16), mo_ref[...])

    def weights(blk, kb, diag, log_beta, after):
        a = jnp.exp2((log_beta + after) * LOG2E)
        if diag:
            a = jnp.where(strict, a, 0.0)
        return _dot(a.astype(BF16), v_ref[rows(kb), :])

    def accumulate(blk, kb, diag, pv):
        acc[blk] = pv if diag else acc[blk] + pv
        if kb == 0:
            out = acc.pop(blk)
            o_ref[rows(blk), :] = jnp.where(first, out[:SB_BLOCK, :], out[SB_BLOCK:, :])

    stages = (scores, logs, weights, accumulate)
    live = {}
    for step in range(len(tiles) + SB_STAGE_STEP[-1]):
        for depth in SB_STAGE_ORDER:
            t = step - SB_STAGE_STEP[depth]
            if 0 <= t < len(tiles):
                prev = live.pop((t, depth - 1), ())
                out = stages[depth](*tiles[t], *prev)
                live[(t, depth)] = out if isinstance(out, tuple) else (out,)


def _stick_breaking(sq, sk, sv, mo, batch, seq):
    t = sq.shape[0]
    nq = seq // SB_BLOCK
    pairs = SB_WIDTH // LANES
    spec = pl.BlockSpec((seq, LANES), lambda b, c: (b, c))
    return pl.pallas_call(
        _stick_breaking_kernel,
        grid=(batch, pairs),
        in_specs=[spec, spec, spec, pl.BlockSpec(mo.shape, lambda b, c: (0, 0))],
        out_specs=spec,
        out_shape=jax.ShapeDtypeStruct((t, SB_WIDTH), F32),
        scratch_shapes=[pltpu.VMEM((nq, 2 * SB_BLOCK, LANES), BF16)],
        compiler_params=_params("parallel", "parallel"),
        name="stick_breaking",
    )(sq, sk, sv, mo)


def _cumsum_matrix():
    j = jnp.arange(SB_BLOCK)[:, None]
    s = jnp.arange(SB_BLOCK)[None, :]
    return (j > s).astype(BF16)


def _mem_kv_kernel(m_ref, g_ref, w_ref, gk_ref, k_ref, v_ref):
    d = m_ref.shape[1]
    hd = d // X_HEADS
    mn = _rms(m_ref[...], g_ref[...]).astype(BF16)
    k = _dot(mn, _bf16(w_ref[:, :d]))
    for h in range(X_HEADS):
        kh = k[:, h * hd:(h + 1) * hd]
        k_ref[:, h * hd:(h + 1) * hd] = (_rms(kh, gk_ref[...]) * (hd ** -0.5)).astype(BF16)
    v_ref[...] = _dot(mn, _bf16(w_ref[:, d:])).astype(BF16)


def _mem_kv(mem, g, w, gk, l):
    t, d = mem.shape
    row = lambda i: (i, 0)
    out = pl.BlockSpec((ROW_TILE, d), row)
    return pl.pallas_call(
        _mem_kv_kernel,
        grid=(t // ROW_TILE,),
        in_specs=[pl.BlockSpec((ROW_TILE, d), row), _layer(g, l), _layer(w, l), _layer(gk, l)],
        out_specs=[out, out],
        out_shape=[jax.ShapeDtypeStruct((t, d), BF16)] * 2,
        compiler_params=_params("parallel"),
        name="mem_kv",
    )(mem, g, w, gk)


def _mix_cross_kernel(x_ref, oret_ref, osb_ref, gsb_ref, wmix_ref, gc_ref, wxq_ref, gq_ref,
                      k_ref, v_ref, wxo_ref, o_ref):
    d = x_ref.shape[1]
    hd = d // X_HEADS
    nsb = _rms(osb_ref[...], gsb_ref[...]).astype(BF16)
    x = (x_ref[...] + _dot(oret_ref[...], _bf16(wmix_ref[:RET_WIDTH, :]))
         + _dot(nsb, _bf16(wmix_ref[RET_WIDTH:, :])))
    q = _dot(_rms(x, gc_ref[...]).astype(BF16), _bf16(wxq_ref[...]))
    heads = []
    for h in range(X_HEADS):
        cols = slice(h * hd, (h + 1) * hd)
        qn = _rms(q[:, cols], gq_ref[...]).astype(BF16)
        s = _nt_dot(qn, k_ref[0, :, cols])
        e = jnp.exp(s - jnp.max(s, axis=-1, keepdims=True))
        p = (e / jnp.sum(e, axis=-1, keepdims=True)).astype(BF16)
        heads.append(_dot(p, v_ref[0, :, cols]).astype(BF16))
    o_ref[...] = x + _dot(jnp.concatenate(heads, axis=1), _bf16(wxo_ref[...]))


def _mix_cross(x, oret, osb, gsb, wmix, gc, wxq, gq, k, v, wxo, l, seq):
    t, d = x.shape
    per_batch = seq // ROW_TILE
    row = lambda i: (i, 0)
    full = lambda a: _layer(a, l)
    mem = pl.BlockSpec((1,) + k.shape[1:], lambda i: (i // per_batch, 0, 0))
    return pl.pallas_call(
        _mix_cross_kernel,
        grid=(t // ROW_TILE,),
        in_specs=[pl.BlockSpec((ROW_TILE, d), row), pl.BlockSpec((ROW_TILE, RET_WIDTH), row),
                  pl.BlockSpec((ROW_TILE, SB_WIDTH), row), full(gsb), full(wmix), full(gc),
                  full(wxq), full(gq), mem, mem, full(wxo)],
        out_specs=pl.BlockSpec((ROW_TILE, d), row),
        out_shape=jax.ShapeDtypeStruct((t, d), F32),
        compiler_params=_params("parallel"),
        name="mix_cross",
    )(x, oret, osb, gsb, wmix, gc, wxq, gq, k, v, wxo)


def _mlp_kernel(x_ref, g_ref, wup_ref, wdn_ref, o_ref, acc_ref):
    x = x_ref[...]
    hm = _rms(x, g_ref[...]).astype(BF16)
    for f in range(wup_ref.shape[1] // FF_CHUNK):
        cols = slice(f * FF_CHUNK, (f + 1) * FF_CHUNK)
        up = _dot(hm, wup_ref[:, cols])
        hidden = jnp.square(jnp.maximum(up, 0.0)).astype(BF16)
        down = _dot(hidden, wdn_ref[cols, :])
        if f:
            acc_ref[...] += down
        else:
            acc_ref[...] = down
    o_ref[...] = x + acc_ref[...]


def _mlp(x, g, wup, wdn, l):
    t, d = x.shape
    row = lambda i: (i, 0)
    return pl.pallas_call(
        _mlp_kernel,
        grid=(t // MLP_TILE,),
        in_specs=[pl.BlockSpec((MLP_TILE, d), row), _layer(g, l), _resident(wup), _resident(wdn)],
        out_specs=pl.BlockSpec((MLP_TILE, d), row),
        out_shape=jax.ShapeDtypeStruct((t, d), F32),
        scratch_shapes=[pltpu.VMEM((MLP_TILE, d), F32)],
        compiler_params=_params("parallel"),
        name="mlp",
    )(x, g, wup, wdn)


def _rotary_tables(seq):
    half = RET_HD // 2
    inv = 1.0 / (ROPE_BASE ** jnp.linspace(0.0, 1.0, half, dtype=F32))
    ang = jnp.arange(seq, dtype=F32)[:, None] * inv[None, :]
    cos, sin = jnp.cos(ang), jnp.sin(ang)
    return jnp.concatenate([cos, cos], axis=1), jnp.concatenate([-sin, sin], axis=1)


def kernel(x, mem, g_mix, w_in, g_ret_out, g_sb_out, w_mix_out, g_cross, g_mem,
           w_xq, w_xkv, g_qn, g_kn, w_xo, g_mlp, w_up, w_down):
    batch, seq, d = x.shape
    n_mem = mem.shape[1]
    depth = w_in.shape[0]
    cos, sin = _rotary_tables(seq)
    ret_tabs = _retention_tables()
    mo = _cumsum_matrix()
    xt = x.reshape(batch * seq, d)
    memt = mem.reshape(batch * n_mem, d)
    row = lambda g: g.reshape(depth, 1, -1)
    for l in range(depth):
        o_ret, sq, sk, sv, up_bf16, down_bf16 = _in_proj(
            xt, row(g_mix), w_in, l, cos, sin, w_up, w_down, ret_tabs,
            g_ret_out.reshape(depth, RET_HEADS, 1, RET_HD), seq)
        o_sb = _stick_breaking(sq, sk, sv, mo, batch, seq)
        km, vm = _mem_kv(memt, row(g_mem), w_xkv, row(g_kn), l)
        xt = _mix_cross(xt, o_ret, o_sb, row(g_sb_out), w_mix_out, row(g_cross), w_xq, row(g_qn),
                        km.reshape(batch, n_mem, d), vm.reshape(batch, n_mem, d), w_xo, l, seq)
        xt = _mlp(xt, row(g_mlp), up_bf16, down_bf16, l)
    return xt.reshape(batch, seq, d)
```

```python
import functools
import math

import jax
import jax.numpy as jnp
from jax import lax
from jax.experimental import pallas as pl
from jax.experimental.pallas import tpu as pltpu

F32 = jnp.float32
BF16 = jnp.bfloat16

EPS = 1e-6
LOG2E = math.log2(math.e)
ROPE_BASE = 10000.0
CHUNK = 64
RET_HEADS = 4
RET_HD = 128
RET_WIDTH = RET_HEADS * RET_HD
SB_HEADS = 8
SB_HD = 64
SB_WIDTH = SB_HEADS * SB_HD
X_HEADS = 4
LANES = 128
ROW_TILE = 1024
MLP_TILE = 1024
RET_BLOCK = 256
PROJ_BLOCK = 512
SB_BLOCK = 128
FF_CHUNK = 512
SB_STAGE_STEP = (0, 2, 4, 5)
SB_STAGE_ORDER = (0, 1, 2, 3)
VMEM_LIMIT = 56 * 1024 * 1024


def _nt_dot(a, b):
    return lax.dot_general(a, b, (((1,), (1,)), ((), ())), preferred_element_type=F32)


def _dot(a, b):
    return jnp.dot(a, b, preferred_element_type=F32)


def _rms(x, g):
    return x * lax.rsqrt(jnp.mean(x * x, axis=-1, keepdims=True) + EPS) * g


def _params(*sem):
    return pltpu.CompilerParams(dimension_semantics=sem, vmem_limit_bytes=VMEM_LIMIT)


def _resident(a):
    return pl.BlockSpec(a.shape, lambda *_: (0,) * a.ndim, pipeline_mode=pl.Buffered(1))


def _layer(a, l):
    return pl.BlockSpec((None,) + a.shape[1:], lambda *_: (l,) + (0,) * (a.ndim - 1),
                        pipeline_mode=pl.Buffered(1))


def _bf16(w):
    return w.astype(BF16)


def _in_proj_kernel(x_ref, g_ref, w_ref, cos_ref, sin_ref, wup_ref, wdn_ref,
                    dw_ref, qd_ref, kd_ref, cd_ref, gn_ref,
                    oret_ref, sq_ref, sk_ref, sv_ref, upb_ref, dnb_ref, state_ref):
    upb_ref[...] = _bf16(wup_ref[...])
    dnb_ref[...] = _bf16(wdn_ref[...])
    tile = pl.program_id(0) % (cos_ref.shape[0] // ROW_TILE)

    @pl.when(tile == 0)
    def _():
        state_ref[...] = jnp.zeros_like(state_ref)

    h = _rms(x_ref[...], g_ref[...]).astype(BF16)
    heads = [slice(hd * RET_HD, (hd + 1) * RET_HD) for hd in range(RET_HEADS)]

    def project(blk):
        rows = slice(blk * PROJ_BLOCK, (blk + 1) * PROJ_BLOCK)
        pos = pl.ds(pl.multiple_of(tile * ROW_TILE + blk * PROJ_BLOCK, PROJ_BLOCK), PROJ_BLOCK)
        cos = cos_ref[pos, :]
        sin = sin_ref[pos, :]
        hb = h[rows]
        proj = lambda i: _dot(hb, _bf16(w_ref[:, i * RET_WIDTH:(i + 1) * RET_WIDTH]))
        rot = lambda c: c * cos + pltpu.roll(c, RET_HD // 2, 1) * sin
        p = proj(0)
        q = [rot(p[:, c]).astype(BF16) for c in heads]
        p = proj(1)
        k = [(rot(p[:, c]) * (RET_HD ** -0.5)).astype(BF16) for c in heads]
        v = proj(2).astype(BF16)
        gate = proj(3)
        sq_ref[rows, :] = (proj(4) * (SB_HD ** -0.5)).astype(BF16)
        sk_ref[rows, :] = proj(5).astype(BF16)
        sv_ref[rows, :] = proj(6).astype(BF16)
        return q, k, v, gate

    def retain(blk, q, k, v, gate):
        for sub in range(PROJ_BLOCK // RET_BLOCK):
            part = slice(sub * RET_BLOCK, (sub + 1) * RET_BLOCK)
            rows = slice(blk * PROJ_BLOCK + part.start, blk * PROJ_BLOCK + part.stop)
            for hd, cols in enumerate(heads):
                qb, kb, vb = q[hd][part], k[hd][part], v[part, cols]
                state = state_ref[hd]
                p = (_nt_dot(qb, kb) * dw_ref[hd]).astype(BF16)
                o = _dot(p, vb) + _dot(qb, state.astype(BF16)) * qd_ref[hd]
                kd = (kb.astype(F32) * kd_ref[hd]).T.astype(BF16)
                state_ref[hd] = state * cd_ref[hd] + _dot(kd, vb)
                gt = gate[part, cols]
                oret_ref[rows, cols] = (_rms(o, gn_ref[hd])
                                        * (gt * jax.nn.sigmoid(gt))).astype(BF16)

    blocks = ROW_TILE // PROJ_BLOCK
    ready = project(0)
    for blk in range(blocks):
        ahead = project(blk + 1) if blk + 1 < blocks else None
        retain(blk, *ready)
        ready = ahead


def _in_proj(x, g, w, l, cos, sin, wup, wdn, tabs, gn, seq):
    t, d = x.shape
    steps = t // ROW_TILE
    ff = wup.shape[2]
    slab = ff // steps
    row = lambda i: (i, 0)
    half = pl.BlockSpec((ROW_TILE, RET_WIDTH), row)
    outs = [jax.ShapeDtypeStruct((t, RET_WIDTH), BF16)] * 4
    outs += [jax.ShapeDtypeStruct((d, ff), BF16), jax.ShapeDtypeStruct((ff, d), BF16)]
    return pl.pallas_call(
        _in_proj_kernel,
        grid=(steps,),
        in_specs=[pl.BlockSpec((ROW_TILE, d), row), _layer(g, l), _layer(w, l),
                  _resident(cos), _resident(sin),
                  pl.BlockSpec((None, d, slab), lambda i: (l, 0, i)),
                  pl.BlockSpec((None, slab, d), lambda i: (l, i, 0))]
        + [_resident(a) for a in tabs] + [_layer(gn, l)],
        out_specs=[half] * 4 + [pl.BlockSpec((d, slab), lambda i: (0, i)),
                                pl.BlockSpec((slab, d), lambda i: (i, 0))],
        out_shape=outs,
        scratch_shapes=[pltpu.VMEM((RET_HEADS, RET_HD, RET_HD), F32)],
        compiler_params=_params("arbitrary"),
        name="in_proj",
    )(x, g, w, cos, sin, wup, wdn, *tabs, gn)


def _retention_tables():
    lg = jnp.log1p(-jnp.exp2(-5.0 - jnp.arange(RET_HEADS, dtype=F32)))[:, None, None]
    idx = jnp.arange(RET_BLOCK, dtype=F32)
    dist = jnp.abs(idx[:, None] - idx[None, :])
    chunk = jnp.arange(RET_BLOCK) // CHUNK
    visible = chunk[None, :] <= chunk[:, None]
    w = jnp.where(visible[None], jnp.exp(lg * dist[None]), 0.0)
    ones = jnp.ones((1, 1, RET_HD), F32)
    qd = jnp.exp(lg * (idx + 1.0)[None, :, None]) * ones
    kd = jnp.exp(lg * (RET_BLOCK - 1.0 - idx)[None, :, None]) * ones
    cd = jnp.exp(lg * RET_BLOCK) * ones
    return w, qd, kd, cd


def _stick_breaking_kernel(q_ref, k_ref, v_ref, mo_ref, o_ref, q2_ref):
    nq = q_ref.shape[0] // SB_BLOCK
    lane = lax.broadcasted_iota(jnp.int32, (SB_BLOCK, LANES), 1)
    first = lane < SB_HD
    rows = lambda blk: slice(blk * SB_BLOCK, (blk + 1) * SB_BLOCK)
    for blk in range(nq):
        q = q_ref[rows(blk), :].astype(F32)
        q2_ref[blk] = jnp.concatenate([jnp.where(first, q, 0.0), jnp.where(first, 0.0, q)],
                                      axis=0).astype(BF16)
    row = lax.broadcasted_iota(jnp.int32, (2 * SB_BLOCK, SB_BLOCK), 0) % SB_BLOCK
    col = lax.broadcasted_iota(jnp.int32, (2 * SB_BLOCK, SB_BLOCK), 1)
    strict = col < row

    tiles = [(blk, kb, kb == blk) for blk in range(nq) for kb in range(blk, -1, -1)]
    carry = {}
    acc = {}

    def scores(blk, kb, diag):
        return _nt_dot(q2_ref[blk], k_ref[rows(kb), :])

    def logs(blk, kb, diag, z):
        log_beta = jnp.minimum(z, 0.0) - jnp.log(1.0 + jnp.exp2(jnp.abs(z) * -LOG2E))
        log_1m = log_beta - z
        if diag:
            log_1m = jnp.where(strict, log_1m, 0.0)
        if not diag:
            log_beta = log_beta + carry[blk]
        if kb:
            total = jnp.sum(log_1m, axis=-1, keepdims=True)
            carry[blk] = total if diag else carry[blk] + total
        return log_beta, _dot(log_1m.astype(BF16), mo_ref[...])

    def weights(blk, kb, diag, log_beta, after):
        a = jnp.exp2((log_beta + after) * LOG2E)
        if diag:
            a = jnp.where(strict, a, 0.0)
        return _dot(a.astype(BF16), v_ref[rows(kb), :])

    def accumulate(blk, kb, diag, pv):
        acc[blk] = pv if diag else acc[blk] + pv
        if kb == 0:
            out = acc.pop(blk)
            o_ref[rows(blk), :] = jnp.where(first, out[:SB_BLOCK, :], out[SB_BLOCK:, :])

    stages = (scores, logs, weights, accumulate)
    live = {}
    for step in range(len(tiles) + SB_STAGE_STEP[-1]):
        for depth in SB_STAGE_ORDER:
            t = step - SB_STAGE_STEP[depth]
            if 0 <= t < len(tiles):
                prev = live.pop((t, depth - 1), ())
                out = stages[depth](*tiles[t], *prev)
                live[(t, depth)] = out if isinstance(out, tuple) else (out,)


def _stick_breaking(sq, sk, sv, mo, batch, seq):
    t = sq.shape[0]
    nq = seq // SB_BLOCK
    pairs = SB_WIDTH // LANES
    spec = pl.BlockSpec((seq, LANES), lambda b, c: (b, c))
    return pl.pallas_call(
        _stick_breaking_kernel,
        grid=(batch, pairs),
        in_specs=[spec, spec, spec, pl.BlockSpec(mo.shape, lambda b, c: (0, 0))],
        out_specs=spec,
        out_shape=jax.ShapeDtypeStruct((t, SB_WIDTH), F32),
        scratch_shapes=[pltpu.VMEM((nq, 2 * SB_BLOCK, LANES), BF16)],
        compiler_params=_params("parallel", "parallel"),
        name="stick_breaking",
    )(sq, sk, sv, mo)


def _cumsum_matrix():
    j = jnp.arange(SB_BLOCK)[:, None]
    s = jnp.arange(SB_BLOCK)[None, :]
    return (j > s).astype(BF16)


def _mem_kv_kernel(m_ref, g_ref, w_ref, gk_ref, k_ref, v_ref):
    d = m_ref.shape[1]
    hd = d // X_HEADS
    mn = _rms(m_ref[...], g_ref[...]).astype(BF16)
    k = _dot(mn, _bf16(w_ref[:, :d]))
    for h in range(X_HEADS):
        kh = k[:, h * hd:(h + 1) * hd]
        k_ref[:, h * hd:(h + 1) * hd] = (_rms(kh, gk_ref[...]) * (hd ** -0.5)).astype(BF16)
    v_ref[...] = _dot(mn, _bf16(w_ref[:, d:])).astype(BF16)


def _mem_kv(mem, g, w, gk):
    t, d = mem.shape
    depth = w.shape[0]
    per_layer = lambda a: pl.BlockSpec((None,) + a.shape[1:],
                                       lambda l, i: (l,) + (0,) * (a.ndim - 1))
    out = pl.BlockSpec((None, ROW_TILE, d), lambda l, i: (l, i, 0))
    return pl.pallas_call(
        _mem_kv_kernel,
        grid=(depth, t // ROW_TILE),
        in_specs=[pl.BlockSpec((ROW_TILE, d), lambda l, i: (i, 0)),
                  per_layer(g), per_layer(w), per_layer(gk)],
        out_specs=[out, out],
        out_shape=[jax.ShapeDtypeStruct((depth, t, d), BF16)] * 2,
        compiler_params=_params("parallel", "parallel"),
        name="mem_kv",
    )(mem, g, w, gk)


def _mix_cross_kernel(x_ref, oret_ref, osb_ref, gsb_ref, wmix_ref, gc_ref, wxq_ref, gq_ref,
                      k_ref, v_ref, wxo_ref, o_ref):
    d = x_ref.shape[1]
    hd = d // X_HEADS
    nsb = _rms(osb_ref[...], gsb_ref[...]).astype(BF16)
    x = (x_ref[...] + _dot(oret_ref[...], _bf16(wmix_ref[:RET_WIDTH, :]))
         + _dot(nsb, _bf16(wmix_ref[RET_WIDTH:, :])))
    q = _dot(_rms(x, gc_ref[...]).astype(BF16), _bf16(wxq_ref[...]))
    heads = []
    for h in range(X_HEADS):
        cols = slice(h * hd, (h + 1) * hd)
        qn = _rms(q[:, cols], gq_ref[...]).astype(BF16)
        s = _nt_dot(qn, k_ref[0, :, cols])
        e = jnp.exp(s - jnp.max(s, axis=-1, keepdims=True))
        p = (e / jnp.sum(e, axis=-1, keepdims=True)).astype(BF16)
        heads.append(_dot(p, v_ref[0, :, cols]).astype(BF16))
    o_ref[...] = x + _dot(jnp.concatenate(heads, axis=1), _bf16(wxo_ref[...]))


def _mix_cross(x, oret, osb, gsb, wmix, gc, wxq, gq, k, v, wxo, l, seq):
    t, d = x.shape
    per_batch = seq // ROW_TILE
    row = lambda i: (i, 0)
    full = lambda a: _layer(a, l)
    mem = pl.BlockSpec((None, 1) + k.shape[2:], lambda i: (l, i // per_batch, 0, 0))
    return pl.pallas_call(
        _mix_cross_kernel,
        grid=(t // ROW_TILE,),
        in_specs=[pl.BlockSpec((ROW_TILE, d), row), pl.BlockSpec((ROW_TILE, RET_WIDTH), row),
                  pl.BlockSpec((ROW_TILE, SB_WIDTH), row), full(gsb), full(wmix), full(gc),
                  full(wxq), full(gq), mem, mem, full(wxo)],
        out_specs=pl.BlockSpec((ROW_TILE, d), row),
        out_shape=jax.ShapeDtypeStruct((t, d), F32),
        compiler_params=_params("parallel"),
        name="mix_cross",
    )(x, oret, osb, gsb, wmix, gc, wxq, gq, k, v, wxo)


def _mlp_kernel(x_ref, g_ref, wup_ref, wdn_ref, o_ref, acc_ref):
    x = x_ref[...]
    hm = _rms(x, g_ref[...]).astype(BF16)
    for f in range(wup_ref.shape[1] // FF_CHUNK):
        cols = slice(f * FF_CHUNK, (f + 1) * FF_CHUNK)
        up = _dot(hm, wup_ref[:, cols])
        hidden = jnp.square(jnp.maximum(up, 0.0)).astype(BF16)
        down = _dot(hidden, wdn_ref[cols, :])
        if f:
            acc_ref[...] += down
        else:
            acc_ref[...] = down
    o_ref[...] = x + acc_ref[...]


def _mlp(x, g, wup, wdn, l):
    t, d = x.shape
    row = lambda i: (i, 0)
    return pl.pallas_call(
        _mlp_kernel,
        grid=(t // MLP_TILE,),
        in_specs=[pl.BlockSpec((MLP_TILE, d), row), _layer(g, l), _resident(wup), _resident(wdn)],
        out_specs=pl.BlockSpec((MLP_TILE, d), row),
        out_shape=jax.ShapeDtypeStruct((t, d), F32),
        scratch_shapes=[pltpu.VMEM((MLP_TILE, d), F32)],
        compiler_params=_params("parallel"),
        name="mlp",
    )(x, g, wup, wdn)


def _rotary_tables(seq):
    half = RET_HD // 2
    inv = 1.0 / (ROPE_BASE ** jnp.linspace(0.0, 1.0, half, dtype=F32))
    ang = jnp.arange(seq, dtype=F32)[:, None] * inv[None, :]
    cos, sin = jnp.cos(ang), jnp.sin(ang)
    return jnp.concatenate([cos, cos], axis=1), jnp.concatenate([-sin, sin], axis=1)


def kernel(x, mem, g_mix, w_in, g_ret_out, g_sb_out, w_mix_out, g_cross, g_mem,
           w_xq, w_xkv, g_qn, g_kn, w_xo, g_mlp, w_up, w_down):
    batch, seq, d = x.shape
    n_mem = mem.shape[1]
    depth = w_in.shape[0]
    cos, sin = _rotary_tables(seq)
    ret_tabs = _retention_tables()
    mo = _cumsum_matrix()
    xt = x.reshape(batch * seq, d)
    memt = mem.reshape(batch * n_mem, d)
    row = lambda g: g.reshape(depth, 1, -1)
    km, vm = _mem_kv(memt, row(g_mem), w_xkv, row(g_kn))
    km = km.reshape(depth, batch, n_mem, d)
    vm = vm.reshape(depth, batch, n_mem, d)
    for l in range(depth):
        o_ret, sq, sk, sv, up_bf16, down_bf16 = _in_proj(
            xt, row(g_mix), w_in, l, cos, sin, w_up, w_down, ret_tabs,
            g_ret_out.reshape(depth, RET_HEADS, 1, RET_HD), seq)
        o_sb = _stick_breaking(sq, sk, sv, mo, batch, seq)
        xt = _mix_cross(xt, o_ret, o_sb, row(g_sb_out), w_mix_out, row(g_cross), w_xq, row(g_qn),
                        km, vm, w_xo, l, seq)
        xt = _mlp(xt, row(g_mlp), up_bf16, down_bf16, l)
    return xt.reshape(batch, seq, d)
```
